```python
import math
import jax, jax.numpy as jnp
from jax import lax
import numpy as np

D_MODEL = 4096
BATCH = 1
SEQ = 8192
DEPTH = 4

N_MIXERS = 3
N_A_LAYERS = (DEPTH + 2) // 3
N_B_LAYERS = (DEPTH + 1) // 3
N_C_LAYERS = DEPTH // 3

D_FF = -(-8 * D_MODEL // (3 * 256)) * 256
RMS_EPS = 1e-6
LN_EPS = 1e-5

A_CHUNK = 128
A_WIDTH = D_MODEL
A_GROUPS = 8
A_GROUP_DIM = A_WIDTH // A_GROUPS

B_WIDTH = D_MODEL // 2
B_GROUP_CH = 16
B_GROUPS = B_WIDTH // B_GROUP_CH
B_STATE = 64
B_DT_MIN = 1e-3
B_DT_MAX = 1e-1

C_HEAD_DIM = 128
C_HEADS = D_MODEL // (2 * C_HEAD_DIM)
C_WIDTH = C_HEADS * C_HEAD_DIM
C_PATTERNS = ((128, 1), (512, 4), (2048, 16))
N_PATTERNS = len(C_PATTERNS)
C_QBLOCK = 128
C_MAX_DIL = max(d for _, d in C_PATTERNS)

kernel_name = 'hybrid_gmlp_s5_dilated_attn_trunk'


def rmsnorm(x, gain):
    xf = x.astype(jnp.float32)
    y = xf * lax.rsqrt(jnp.mean(xf * xf, axis=-1, keepdims=True) + RMS_EPS)
    return (y * gain.astype(jnp.float32)).astype(x.dtype)


def swiglu_ffn(h, w_gate, w_up, w_down):
    return (jax.nn.silu(h @ w_gate) * (h @ w_up)) @ w_down


def gmlp_chunk_mixer(h, w_in, v_gain, w_spatial, b_spatial, w_out):
    bsz, seq, _ = h.shape
    u, v = jnp.split(jax.nn.gelu(h @ w_in), 2, axis=-1)
    vf = v.astype(jnp.float32)
    vf = vf - jnp.mean(vf, axis=-1, keepdims=True)
    vf = vf * lax.rsqrt(jnp.mean(vf * vf, axis=-1, keepdims=True) + LN_EPS)
    v = (vf * v_gain.astype(jnp.float32)).astype(h.dtype)
    v = v.reshape(bsz, seq // A_CHUNK, A_CHUNK, A_GROUPS, A_GROUP_DIM)
    causal = jnp.tril(jnp.ones((A_CHUNK, A_CHUNK), dtype=bool))
    w_c = jnp.where(causal, w_spatial, jnp.zeros_like(w_spatial))
    mixed = jnp.einsum('gts,bcsgd->bctgd', w_c, v) + b_spatial.T[None, None, :, :, None]
    return (u * mixed.reshape(bsz, seq, A_WIDTH)) @ w_out


def _cmul(ar, ai, br, bi):
    return ar * br - ai * bi, ar * bi + ai * br


def _scan_combine(e1, e2):
    a1r, a1i, b1r, b1i = e1
    a2r, a2i, b2r, b2i = e2
    ar, ai = _cmul(a2r, a2i, a1r, a1i)
    br, bi = _cmul(a2r, a2i, b1r, b1i)
    return ar, ai, br + b2r, bi + b2i


def s5_mixer(h, w_in, lam_re, lam_im, log_step, b_re, b_im, c_re, c_im, d_skip, w_out):
    bsz, seq, _ = h.shape
    f32 = jnp.float32
    u = (h @ w_in).astype(f32).reshape(bsz, seq, B_GROUPS, B_GROUP_CH)
    lr, li = lam_re.astype(f32), lam_im.astype(f32)
    dt = jnp.exp(log_step.astype(f32))[:, None]
    decay = jnp.exp(lr * dt)
    ab_re, ab_im = decay * jnp.cos(li * dt), decay * jnp.sin(li * dt)
    den = lr * lr + li * li
    nr, ni = ab_re - 1.0, ab_im
    coef_re = (nr * lr + ni * li) / den
    coef_im = (ni * lr - nr * li) / den
    bb_re, bb_im = _cmul(coef_re[..., None], coef_im[..., None], b_re.astype(f32), b_im.astype(f32))
    bu_re = jnp.einsum('btgc,gpc->btgp', u, bb_re)
    bu_im = jnp.einsum('btgc,gpc->btgp', u, bb_im)
    a_re = jnp.broadcast_to(ab_re, bu_re.shape)
    a_im = jnp.broadcast_to(ab_im, bu_im.shape)
    _, _, xr, xi = lax.associative_scan(_scan_combine, (a_re, a_im, bu_re, bu_im), axis=1)
    y = (jnp.einsum('btgp,gcp->btgc', xr, c_re.astype(f32))
         - jnp.einsum('btgp,gcp->btgc', xi, c_im.astype(f32))
         + d_skip.astype(f32).reshape(B_GROUPS, B_GROUP_CH) * u)
    y = jax.nn.gelu(y).reshape(bsz, seq, B_WIDTH).astype(h.dtype)
    val, gate = jnp.split(y @ w_out, 2, axis=-1)
    return val * jax.nn.sigmoid(gate)


def strided_window_attention(q, k, v, n_back, dil):
    assert n_back <= C_QBLOCK
    bsz, tp, nh, dh = q.shape
    nb = tp // (dil * C_QBLOCK)

    def to_blocks(a):
        return a.reshape(bsz, nb, C_QBLOCK, dil, nh, dh)

    qb, kb, vb = to_blocks(q), to_blocks(k), to_blocks(v)
    pad6 = ((0, 0), (1, 0), (0, 0), (0, 0), (0, 0), (0, 0))
    kcat = jnp.concatenate([jnp.pad(kb[:, :-1], pad6), kb], axis=2)
    vcat = jnp.concatenate([jnp.pad(vb[:, :-1], pad6), vb], axis=2)
    scores = jnp.einsum('bnirhd,bnjrhd->bnrhij', qb, kcat,
                        preferred_element_type=jnp.float32) * (dh ** -0.5)
    qi = jnp.arange(C_QBLOCK)[:, None]
    kj = jnp.arange(2 * C_QBLOCK)[None, :]
    dist = C_QBLOCK + qi - kj
    band = (dist >= 0) & (dist <= n_back)
    not_first = (jnp.arange(nb) > 0)[:, None, None]
    valid = band[None] & (not_first | (kj >= C_QBLOCK)[None])
    scores = jnp.where(valid[None, :, None, None], scores, -jnp.inf)
    m = jnp.max(scores, axis=-1, keepdims=True)
    e = jnp.exp(scores - m)
    s = jnp.sum(e, axis=-1, keepdims=True)
    p = e / s
    lse = (m + jnp.log(s))[..., 0]
    o = jnp.einsum('bnrhij,bnjrhd->bnirhd', p.astype(v.dtype), vcat)
    o = o.reshape(bsz, tp, nh, dh)
    lse = jnp.transpose(lse, (0, 1, 4, 2, 3)).reshape(bsz, tp, nh)
    return o, lse


def dilated_attention_mixer(h, w_in, w_out):
    bsz, seq, _ = h.shape
    qkv = (h @ w_in).reshape(bsz, seq, N_PATTERNS, 3, C_HEADS, C_HEAD_DIM)
    pad = (-seq) % (C_QBLOCK * C_MAX_DIL)
    qkv = jnp.pad(qkv, ((0, 0), (0, pad), (0, 0), (0, 0), (0, 0), (0, 0)))
    outs, lses = [], []
    for p_idx, (window, dil) in enumerate(C_PATTERNS):
        o, lse = strided_window_attention(qkv[:, :, p_idx, 0], qkv[:, :, p_idx, 1],
                                          qkv[:, :, p_idx, 2], window // dil, dil)
        outs.append(o)
        lses.append(lse)
    alpha = jax.nn.softmax(jnp.stack(lses), axis=0)
    o_all = jnp.stack(outs).astype(jnp.float32)
    out = jnp.sum(alpha[..., None] * o_all, axis=0)[:, :seq].astype(h.dtype)
    return out.reshape(bsz, seq, C_WIDTH) @ w_out


def setup_inputs(seed: int = 0) -> dict:
    key = jax.random.key(seed)
    ks = jax.random.split(key, 24)
    nrm = jax.random.normal
    f32 = jnp.float32
    x = nrm(ks[0], (BATCH, SEQ, D_MODEL), f32)
    norm_mix = 1.0 + 0.02 * nrm(ks[1], (DEPTH, D_MODEL), f32)
    norm_ffn = 1.0 + 0.02 * nrm(ks[2], (DEPTH, D_MODEL), f32)
    w_gate = nrm(ks[3], (DEPTH, D_MODEL, D_FF), f32) * D_MODEL ** -0.5
    w_up = nrm(ks[4], (DEPTH, D_MODEL, D_FF), f32) * D_MODEL ** -0.5
    w_down = nrm(ks[5], (DEPTH, D_FF, D_MODEL), f32) * D_FF ** -0.5
    a_w_in = nrm(ks[6], (N_A_LAYERS, D_MODEL, 2 * A_WIDTH), f32) * D_MODEL ** -0.5
    a_v_gain = 1.0 + 0.02 * nrm(ks[7], (N_A_LAYERS, A_WIDTH), f32)
    a_w_spatial = nrm(ks[8], (N_A_LAYERS, A_GROUPS, A_CHUNK, A_CHUNK), f32) * A_CHUNK ** -0.5
    a_b_spatial = 1.0 + 0.1 * nrm(ks[9], (N_A_LAYERS, A_GROUPS, A_CHUNK), f32)
    a_w_out = nrm(ks[10], (N_A_LAYERS, A_WIDTH, D_MODEL), f32) * A_WIDTH ** -0.5
    b_w_in = nrm(ks[11], (N_B_LAYERS, D_MODEL, B_WIDTH), f32) * D_MODEL ** -0.5
    b_lambda_re = -0.5 + 0.01 * nrm(ks[12], (N_B_LAYERS, B_GROUPS, B_STATE), f32)
    b_lambda_im = (math.pi * jnp.arange(B_STATE, dtype=f32)
                   + 0.01 * nrm(ks[13], (N_B_LAYERS, B_GROUPS, B_STATE), f32))
    b_log_step = jax.random.uniform(ks[14], (N_B_LAYERS, B_GROUPS), f32,
                                    minval=math.log(B_DT_MIN), maxval=math.log(B_DT_MAX))
    b_b_re = nrm(ks[15], (N_B_LAYERS, B_GROUPS, B_STATE, B_GROUP_CH), f32) * (2 * B_GROUP_CH) ** -0.5
    b_b_im = nrm(ks[16], (N_B_LAYERS, B_GROUPS, B_STATE, B_GROUP_CH), f32) * (2 * B_GROUP_CH) ** -0.5
    b_c_re = nrm(ks[17], (N_B_LAYERS, B_GROUPS, B_GROUP_CH, B_STATE), f32) * B_STATE ** -0.5
    b_c_im = nrm(ks[18], (N_B_LAYERS, B_GROUPS, B_GROUP_CH, B_STATE), f32) * B_STATE ** -0.5
    b_d_skip = nrm(ks[19], (N_B_LAYERS, B_WIDTH), f32)
    b_w_out = nrm(ks[20], (N_B_LAYERS, B_WIDTH, 2 * D_MODEL), f32) * B_WIDTH ** -0.5
    c_w_in = nrm(ks[21], (N_C_LAYERS, D_MODEL, N_PATTERNS * 3 * C_WIDTH), f32) * D_MODEL ** -0.5
    c_w_out = nrm(ks[22], (N_C_LAYERS, C_WIDTH, D_MODEL), f32) * C_WIDTH ** -0.5
    final_norm = 1.0 + 0.02 * nrm(ks[23], (D_MODEL,), f32)
    return {'x': x, 'norm_mix': norm_mix, 'norm_ffn': norm_ffn, 'w_gate': w_gate, 'w_up': w_up,
            'w_down': w_down, 'a_w_in': a_w_in, 'a_v_gain': a_v_gain, 'a_w_spatial': a_w_spatial,
            'a_b_spatial': a_b_spatial, 'a_w_out': a_w_out, 'b_w_in': b_w_in,
            'b_lambda_re': b_lambda_re, 'b_lambda_im': b_lambda_im, 'b_log_step': b_log_step,
            'b_b_re': b_b_re, 'b_b_im': b_b_im, 'b_c_re': b_c_re, 'b_c_im': b_c_im,
            'b_d_skip': b_d_skip, 'b_w_out': b_w_out, 'c_w_in': c_w_in, 'c_w_out': c_w_out,
            'final_norm': final_norm}


def reference(x, norm_mix, norm_ffn, w_gate, w_up, w_down, a_w_in, a_v_gain, a_w_spatial,
              a_b_spatial, a_w_out, b_w_in, b_lambda_re, b_lambda_im, b_log_step, b_b_re, b_b_im,
              b_c_re, b_c_im, b_d_skip, b_w_out, c_w_in, c_w_out, final_norm):
    h = x
    for i in range(DEPTH):
        kind, j = i % N_MIXERS, i // N_MIXERS
        hn = rmsnorm(h, norm_mix[i])
        if kind == 0:
            mix = gmlp_chunk_mixer(hn, a_w_in[j], a_v_gain[j], a_w_spatial[j], a_b_spatial[j], a_w_out[j])
        elif kind == 1:
            mix = s5_mixer(hn, b_w_in[j], b_lambda_re[j], b_lambda_im[j], b_log_step[j], b_b_re[j],
                           b_b_im[j], b_c_re[j], b_c_im[j], b_d_skip[j], b_w_out[j])
        else:
            mix = dilated_attention_mixer(hn, c_w_in[j], c_w_out[j])
        h = h + mix
        h = h + swiglu_ffn(rmsnorm(h, norm_ffn[i]), w_gate[i], w_up[i], w_down[i])
    return rmsnorm(h, final_norm)
```

```python
import functools
import math

import jax
import jax.numpy as jnp
from jax import lax
from jax.experimental import pallas as pl
from jax.experimental.pallas import tpu as pltpu

F32 = jnp.float32
BF16 = jnp.bfloat16

RMS_EPS = 1e-6
LN_EPS = 1e-5

LANES = 128
SUBLANES = 8
VMEM_LIMIT_BYTES = 56 * 1024 * 1024

A_CHUNK = 128
A_GROUPS = 8
B_GROUP_CH = 16
S5_GROUPS_PER_BLOCK = LANES // B_GROUP_CH
S5_SEGMENTS = SUBLANES
S5_CHUNK_STEPS = 128
C_HEAD_DIM = 128
C_QBLOCK = 128
C_PATTERNS = ((128, 1), (512, 4), (2048, 16))


def _cparams(sem):
    return pltpu.CompilerParams(dimension_semantics=sem, vmem_limit_bytes=VMEM_LIMIT_BYTES)


def _rmsnorm_body(x_ref, g_ref, o_ref):
    x = x_ref[...]
    y = x * lax.rsqrt(jnp.mean(x * x, axis=-1, keepdims=True) + RMS_EPS)
    o_ref[...] = (y * g_ref[...]).astype(o_ref.dtype)


def rmsnorm(x, gain, out_dtype, tm=256):
    t, d = x.shape
    return pl.pallas_call(
        _rmsnorm_body,
        grid=(t // tm,),
        in_specs=[pl.BlockSpec((tm, d), lambda i: (i, 0)),
                  pl.BlockSpec((1, d), lambda i: (0, 0))],
        out_specs=pl.BlockSpec((tm, d), lambda i: (i, 0)),
        out_shape=jax.ShapeDtypeStruct((t, d), out_dtype),
        compiler_params=_cparams(("parallel",)),
        name="rmsnorm",
    )(x, gain.reshape(1, d))


def _mm_body(*refs, n_b, has_res, nk, epilogue):
    a_ref = refs[0]
    b_refs = refs[1:1 + n_b]
    pos = 1 + n_b
    res_ref = refs[pos] if has_res else None
    pos += int(has_res)
    o_ref = refs[pos]
    acc_refs = refs[pos + 1:]
    a = a_ref[...]
    if nk == 1:
        accs = [jnp.dot(a, b[...], preferred_element_type=F32) for b in b_refs]
        res = res_ref[...] if has_res else None
        o_ref[...] = epilogue(accs, res).astype(o_ref.dtype)
        return

    k = pl.program_id(2)

    @pl.when(k == 0)
    def _():
        for acc in acc_refs:
            acc[...] = jnp.zeros_like(acc)

    for acc, b in zip(acc_refs, b_refs):
        acc[...] += jnp.dot(a, b[...], preferred_element_type=F32)

    @pl.when(k == nk - 1)
    def _():
        res = res_ref[...] if has_res else None
        o_ref[...] = epilogue([acc[...] for acc in acc_refs], res).astype(o_ref.dtype)


def matmul(a, bs, col_offsets, n_out, epilogue, out_dtype, *, res=None, tm, tn, tk=None, name):
    m, kdim = a.shape
    tk = kdim if tk is None else tk
    nk = kdim // tk
    assert m % tm == 0 and n_out % tn == 0 and kdim % tk == 0
    assert all(off % tn == 0 for off in col_offsets)
    in_specs = [pl.BlockSpec((tm, tk), lambda i, j, k: (i, k))]
    for off in col_offsets:
        in_specs.append(pl.BlockSpec((tk, tn), lambda i, j, k, o=off // tn: (k, j + o)))
    args = [a] + list(bs)
    if res is not None:
        in_specs.append(pl.BlockSpec((tm, tn), lambda i, j, k: (i, j)))
        args.append(res)
    scratch = [pltpu.VMEM((tm, tn), F32) for _ in bs] if nk > 1 else []
    return pl.pallas_call(
        functools.partial(_mm_body, n_b=len(bs), has_res=res is not None, nk=nk, epilogue=epilogue),
        grid=(m // tm, n_out // tn, nk),
        in_specs=in_specs,
        out_specs=pl.BlockSpec((tm, tn), lambda i, j, k: (i, j)),
        out_shape=jax.ShapeDtypeStruct((m, n_out), out_dtype),
        scratch_shapes=scratch,
        compiler_params=_cparams(("parallel", "parallel", "arbitrary")),
        name=name,
    )(*args)


def _epi_id(accs, res):
    return accs[0]


def _epi_gelu(accs, res):
    return jax.nn.gelu(accs[0])


def _epi_res(accs, res):
    return res + accs[0]


def _epi_swiglu(accs, res):
    return jax.nn.silu(accs[0]) * accs[1]


def _epi_glu_res(accs, res):
    return res + accs[0] * jax.nn.sigmoid(accs[1])


def _gmlp_gate_body(u_ref, v_ref, gain_ref, ws_ref, bs_ref, o_ref, *, n_chunks, group_dim):
    v = v_ref[...].astype(F32)
    v = v - jnp.mean(v, axis=-1, keepdims=True)
    v = v * lax.rsqrt(jnp.mean(v * v, axis=-1, keepdims=True) + LN_EPS)
    vn = (v * gain_ref[...]).astype(BF16)
    row = lax.broadcasted_iota(jnp.int32, (A_CHUNK, A_CHUNK), 0)
    col = lax.broadcasted_iota(jnp.int32, (A_CHUNK, A_CHUNK), 1)
    causal = col <= row
    n_groups = ws_ref.shape[0]
    for g in range(n_groups):
        w = jnp.where(causal, ws_ref[g], 0.0).astype(BF16)
        bias = bs_ref[:, g:g + 1]
        cols = slice(g * group_dim, (g + 1) * group_dim)
        for c in range(n_chunks):
            rows = slice(c * A_CHUNK, (c + 1) * A_CHUNK)
            mixed = jnp.dot(w, vn[rows, cols], preferred_element_type=F32) + bias
            o_ref[rows, cols] = (u_ref[rows, cols].astype(F32) * mixed).astype(o_ref.dtype)


def gmlp_gate(uv, v_gain, w_spatial, b_spatial, n_chunks=2):
    t, w2 = uv.shape
    width = w2 // 2
    tm = n_chunks * A_CHUNK
    n_groups = w_spatial.shape[0]
    return pl.pallas_call(
        functools.partial(_gmlp_gate_body, n_chunks=n_chunks, group_dim=width // n_groups),
        grid=(t // tm,),
        in_specs=[pl.BlockSpec((tm, width), lambda i: (i, 0)),
                  pl.BlockSpec((tm, width), lambda i: (i, 1)),
                  pl.BlockSpec((1, width), lambda i: (0, 0)),
                  pl.BlockSpec((n_groups, A_CHUNK, A_CHUNK), lambda i: (0, 0, 0)),
                  pl.BlockSpec((A_CHUNK, n_groups), lambda i: (0, 0))],
        out_specs=pl.BlockSpec((tm, width), lambda i: (i, 0)),
        out_shape=jax.ShapeDtypeStruct((t, width), BF16),
        compiler_params=_cparams(("parallel",)),
        name="gmlp_gate",
    )(uv, uv, v_gain.reshape(1, width), w_spatial, b_spatial.T)


def _s5_discretize_body(lr_ref, li_ref, ls_ref, br_ref, bi_ref, abr_ref, abi_ref, bbr_ref, bbi_ref):
    lr, li = lr_ref[...], li_ref[...]
    dt = jnp.exp(ls_ref[...])
    decay = jnp.exp(lr * dt)
    ab_re, ab_im = decay * jnp.cos(li * dt), decay * jnp.sin(li * dt)
    den = lr * lr + li * li
    nr, ni = ab_re - 1.0, ab_im
    coef_re = (nr * lr + ni * li) / den
    coef_im = (ni * lr - nr * li) / den
    br, bi = br_ref[...], bi_ref[...]
    abr_ref[...] = ab_re
    abi_ref[...] = ab_im
    bbr_ref[...] = coef_re * br - coef_im * bi
    bbi_ref[...] = coef_re * bi + coef_im * br


def s5_discretize(lam_re, lam_im, log_step, b_re, b_im):
    g, p = lam_re.shape
    rows = g * B_GROUP_CH
    rep = lambda z: jnp.repeat(z, B_GROUP_CH, axis=0)
    b_rows = lambda b: jnp.transpose(b, (0, 2, 1)).reshape(rows, p)
    out = jax.ShapeDtypeStruct((rows, p), F32)
    return pl.pallas_call(
        _s5_discretize_body,
        out_shape=(out, out, out, out),
        name="s5_discretize",
    )(rep(lam_re), rep(lam_im), rep(log_step.reshape(g, 1)), b_rows(b_re), b_rows(b_im))


def _s5_scan_body(u_ref, bblk_ref, cre_ref, cim_ref, ar_ref, ai_ref, d_ref, y_ref,
                  lhs_ref, bu_ref, x_ref, yl_ref, *, steps, n_state):
    ls = S5_CHUNK_STEPS
    n_chunks = steps // ls
    ar = jnp.broadcast_to(ar_ref[...], (S5_SEGMENTS, n_state))
    ai = jnp.broadcast_to(ai_ref[...], (S5_SEGMENTS, n_state))

    def project(c):
        def gather(i, carry):
            lhs_ref[pl.ds(i * S5_SEGMENTS, S5_SEGMENTS), :] = (
                u_ref[pl.ds(c * ls + i, S5_SEGMENTS, stride=steps), :])
            return carry
        lax.fori_loop(0, ls, gather, 0)
        bu_ref[...] = jnp.dot(lhs_ref[...].astype(BF16), bblk_ref[...], preferred_element_type=F32)

    def recur(state, store):
        def step(i, carry):
            xr, xi = carry
            rows = pl.ds(i * S5_SEGMENTS, S5_SEGMENTS)
            nxr = ar * xr - ai * xi + bu_ref[rows, 0:n_state]
            nxi = ar * xi + ai * xr + bu_ref[rows, n_state:2 * n_state]
            if store:
                x_ref[rows, 0:n_state] = nxr
                x_ref[rows, n_state:2 * n_state] = nxi
            return nxr, nxi
        return lax.fori_loop(0, ls, step, state, unroll=2)

    zero = jnp.zeros((S5_SEGMENTS, n_state), F32)

    def pass1(c, state):
        project(c)
        return recur(state, False)
    er, ei = lax.fori_loop(0, n_chunks, pass1, (zero, zero))

    pr, pi = ar_ref[...], ai_ref[...]
    for _ in range(int(math.log2(steps))):
        pr, pi = pr * pr - pi * pi, 2.0 * pr * pi
    cr = jnp.zeros((1, n_state), F32)
    ci = jnp.zeros((1, n_state), F32)
    init_r, init_i = [], []
    for j in range(S5_SEGMENTS):
        init_r.append(cr)
        init_i.append(ci)
        cr, ci = (er[j:j + 1] + pr * cr - pi * ci, ei[j:j + 1] + pr * ci + pi * cr)
    x0 = (jnp.concatenate(init_r, axis=0), jnp.concatenate(init_i, axis=0))

    d = d_ref[...]

    def pass2(c, state):
        project(c)
        state = recur(state, True)
        xr = x_ref[:, 0:n_state].astype(BF16)
        xi = x_ref[:, n_state:2 * n_state].astype(BF16)
        yl_ref[...] = (jnp.dot(xr, cre_ref[...], preferred_element_type=F32)
                       - jnp.dot(xi, cim_ref[...], preferred_element_type=F32))

        def emit(i, carry):
            rows = pl.ds(c * ls + i, S5_SEGMENTS, stride=steps)
            y = yl_ref[pl.ds(i * S5_SEGMENTS, S5_SEGMENTS), :] + d * u_ref[rows, :]
            y_ref[rows, :] = jax.nn.gelu(y)
            return carry
        lax.fori_loop(0, ls, emit, 0)
        return state
    lax.fori_loop(0, n_chunks, pass2, x0)


def s5_scan(u, ab_re, ab_im, bb_re, bb_im, c_re, c_im, d_skip):
    t, width = u.shape
    g, p = ab_re.shape
    gb = S5_GROUPS_PER_BLOCK
    n_blocks = g // gb
    n_state = gb * p
    steps = t // S5_SEGMENTS
    assert steps % S5_CHUNK_STEPS == 0 and steps & (steps - 1) == 0
    eye = jnp.eye(gb, dtype=F32)

    def blockdiag_in(bb):
        z = bb.reshape(n_blocks, gb, B_GROUP_CH, p)
        return jnp.einsum('bgcp,gh->bgchp', z, eye).reshape(n_blocks, gb * B_GROUP_CH, n_state)

    def blockdiag_out(cc):
        z = cc.reshape(n_blocks, gb, B_GROUP_CH, p)
        return jnp.einsum('bgcp,gh->bhpgc', z, eye).reshape(n_blocks, n_state, gb * B_GROUP_CH)

    bblk = jnp.concatenate([blockdiag_in(bb_re), blockdiag_in(bb_im)], axis=-1).astype(BF16)
    cre = blockdiag_out(c_re).astype(BF16)
    cim = blockdiag_out(c_im).astype(BF16)
    ar = ab_re.reshape(n_blocks, 1, n_state)
    ai = ab_im.reshape(n_blocks, 1, n_state)
    d = d_skip.reshape(n_blocks, 1, LANES)
    rows_chunk = S5_CHUNK_STEPS * S5_SEGMENTS
    return pl.pallas_call(
        functools.partial(_s5_scan_body, steps=steps, n_state=n_state),
        grid=(n_blocks,),
        in_specs=[pl.BlockSpec((t, LANES), lambda b: (0, b)),
                  pl.BlockSpec((None, LANES, 2 * n_state), lambda b: (b, 0, 0)),
                  pl.BlockSpec((None, n_state, LANES), lambda b: (b, 0, 0)),
                  pl.BlockSpec((None, n_state, LANES), lambda b: (b, 0, 0)),
                  pl.BlockSpec((None, 1, n_state), lambda b: (b, 0, 0)),
                  pl.BlockSpec((None, 1, n_state), lambda b: (b, 0, 0)),
                  pl.BlockSpec((None, 1, LANES), lambda b: (b, 0, 0))],
        out_specs=pl.BlockSpec((t, LANES), lambda b: (0, b)),
        out_shape=jax.ShapeDtypeStruct((t, width), F32),
        scratch_shapes=[pltpu.VMEM((rows_chunk, LANES), F32),
                        pltpu.VMEM((rows_chunk, 2 * n_state), F32),
                        pltpu.VMEM((rows_chunk, 2 * n_state), F32),
                        pltpu.VMEM((rows_chunk, LANES), F32)],
        compiler_params=_cparams(("parallel",)),
        name="s5_scan",
    )(u, bblk, cre, cim, ar, ai, d)


def _attn_body(q_ref, kp_ref, kc_ref, vp_ref, vc_ref, o_ref, lse_ref, *, blocks_per_residue, n_heads):
    b = pl.program_id(0)
    not_first = (b % blocks_per_residue) != 0
    qi = lax.broadcasted_iota(jnp.int32, (C_QBLOCK, 2 * C_QBLOCK), 0)
    kj = lax.broadcasted_iota(jnp.int32, (C_QBLOCK, 2 * C_QBLOCK), 1)
    dist = C_QBLOCK + qi - kj
    valid = (dist >= 0) & (dist <= C_QBLOCK) & (not_first | (kj >= C_QBLOCK))
    scale = C_HEAD_DIM ** -0.5
    for h in range(n_heads):
        cols = slice(h * C_HEAD_DIM, (h + 1) * C_HEAD_DIM)
        q = q_ref[:, cols]
        k = jnp.concatenate([kp_ref[:, cols], kc_ref[:, cols]], axis=0)
        v = jnp.concatenate([vp_ref[:, cols], vc_ref[:, cols]], axis=0)
        s = lax.dot_general(q, k, (((1,), (1,)), ((), ())), preferred_element_type=F32) * scale
        s = jnp.where(valid, s, -jnp.inf)
        m = jnp.max(s, axis=-1, keepdims=True)
        e = jnp.exp(s - m)
        ssum = jnp.sum(e, axis=-1, keepdims=True)
        p = e / ssum
        o_ref[:, cols] = jnp.dot(p.astype(BF16), v, preferred_element_type=F32)
        lse_ref[:, cols] = jnp.broadcast_to(m + jnp.log(ssum), (C_QBLOCK, C_HEAD_DIM))


def window_attention(qkv, dil):
    t, w3 = qkv.shape
    width = w3 // 3
    n_heads = width // C_HEAD_DIM
    n_blocks = t // C_QBLOCK
    cur = lambda part: pl.BlockSpec((C_QBLOCK, width), lambda b, p=part: (b, p))
    prev = lambda part: pl.BlockSpec((C_QBLOCK, width), lambda b, p=part: (jnp.maximum(b - 1, 0), p))
    out = jax.ShapeDtypeStruct((t, width), F32)
    return pl.pallas_call(
        functools.partial(_attn_body, blocks_per_residue=n_blocks // dil, n_heads=n_heads),
        grid=(n_blocks,),
        in_specs=[cur(0), prev(1), cur(1), prev(2), cur(2)],
        out_specs=(pl.BlockSpec((C_QBLOCK, width), lambda b: (b, 0)),
                   pl.BlockSpec((C_QBLOCK, width), lambda b: (b, 0))),
        out_shape=(out, out),
        compiler_params=_cparams(("parallel",)),
        name=f"window_attention_d{dil}",
    )(qkv, qkv, qkv, qkv, qkv)


def _attn_combine_body(*refs):
    n = (len(refs) - 1) // 2
    o_refs, l_refs, out_ref = refs[:n], refs[n:2 * n], refs[2 * n]
    ls = [l[...] for l in l_refs]
    mx = functools.reduce(jnp.maximum, ls)
    es = [jnp.exp(l - mx) for l in ls]
    den = functools.reduce(lambda a, b: a + b, es)
    num = functools.reduce(lambda a, b: a + b, [(e / den) * o[...] for e, o in zip(es, o_refs)])
    out_ref[...] = num.astype(out_ref.dtype)


def attn_combine(outs, lses, tm=256):
    t, width = outs[0].shape
    spec = pl.BlockSpec((tm, width), lambda i: (i, 0))
    return pl.pallas_call(
        _attn_combine_body,
        grid=(t // tm,),
        in_specs=[spec] * (2 * len(outs)),
        out_specs=spec,
        out_shape=jax.ShapeDtypeStruct((t, width), BF16),
        compiler_params=_cparams(("parallel",)),
        name="attn_combine",
    )(*outs, *lses)


def _to_residue_major(a, dil):
    t, d = a.shape
    return a.reshape(t // dil, dil, d).transpose(1, 0, 2).reshape(t, d)


def _from_residue_major(a, dil):
    t, d = a.shape
    return a.reshape(dil, t // dil, d).transpose(1, 0, 2).reshape(t, d)


def _pad_cols(w, mult):
    pad = (-w.shape[1]) % mult
    return jnp.pad(w, ((0, 0), (0, pad))) if pad else w


def _pad_rows(w, mult):
    pad = (-w.shape[0]) % mult
    return jnp.pad(w, ((0, pad), (0, 0))) if pad else w


def ffn_layer(h, gain, w_gate, w_up, w_down, tm=1024):
    d = h.shape[1]
    hn = rmsnorm(h, gain, BF16)
    wg = _pad_cols(w_gate.astype(BF16), 1024)
    wu = _pad_cols(w_up.astype(BF16), 1024)
    wd = _pad_rows(w_down.astype(BF16), 1024)
    ff = wg.shape[1]
    mid = matmul(hn, [wg, wu], [0, 0], ff, _epi_swiglu, BF16, tm=tm, tn=512, name="ffn_gate_up")
    n_k = 4 if ff % (4 * LANES) == 0 else 1
    return matmul(mid, [wd], [0], d, _epi_res, F32, res=h, tm=tm, tn=min(1024, d), tk=ff // n_k,
                  name="ffn_down")


def gmlp_layer(h, gain, w_in, v_gain, w_spatial, b_spatial, w_out, tm=1024):
    d = h.shape[1]
    hn = rmsnorm(h, gain, BF16)
    uv = matmul(hn, [w_in.astype(BF16)], [0], w_in.shape[1], _epi_gelu, BF16, tm=tm, tn=min(1024, d),
                name="gmlp_in")
    gated = gmlp_gate(uv, v_gain, w_spatial, b_spatial)
    return matmul(gated, [w_out.astype(BF16)], [0], d, _epi_res, F32, res=h, tm=tm, tn=min(1024, d),
                  name="gmlp_out")


def s5_layer(h, gain, w_in, lam_re, lam_im, log_step, b_re, b_im, c_re, c_im, d_skip, w_out, tm=1024):
    d = h.shape[1]
    g, p = lam_re.shape
    hn = rmsnorm(h, gain, BF16)
    u = matmul(hn, [w_in.astype(BF16)], [0], w_in.shape[1], _epi_id, F32, tm=tm,
               tn=min(1024, w_in.shape[1]), name="s5_in")
    ab_re, ab_im, bb_re, bb_im = s5_discretize(lam_re, lam_im, log_step, b_re, b_im)
    y = s5_scan(u, ab_re[::B_GROUP_CH], ab_im[::B_GROUP_CH], bb_re, bb_im, c_re, c_im, d_skip)
    wo = w_out.astype(BF16)
    return matmul(y.astype(BF16), [wo, wo], [0, d], d, _epi_glu_res, F32, res=h, tm=tm,
                  tn=min(1024, d), name="s5_out")


def attn_layer(h, gain, w_in, w_out, tm=1024):
    d = h.shape[1]
    width = w_out.shape[0]
    hn = rmsnorm(h, gain, BF16)
    w_in = w_in.astype(BF16)
    outs, lses = [], []
    for p_idx, (window, dil) in enumerate(C_PATTERNS):
        assert window // dil == C_QBLOCK
        w_p = w_in[:, p_idx * 3 * width:(p_idx + 1) * 3 * width]
        qkv = matmul(_to_residue_major(hn, dil), [w_p], [0], 3 * width, _epi_id, BF16, tm=tm,
                     tn=min(1024, width), name=f"attn_in_d{dil}")
        o, lse = window_attention(qkv, dil)
        outs.append(_from_residue_major(o, dil))
        lses.append(_from_residue_major(lse, dil))
    mixed = attn_combine(outs, lses)
    return matmul(mixed, [w_out.astype(BF16)], [0], d, _epi_res, F32, res=h, tm=tm, tn=min(1024, d),
                  name="attn_out")


def kernel(x, norm_mix, norm_ffn, w_gate, w_up, w_down, a_w_in, a_v_gain, a_w_spatial, a_b_spatial,
           a_w_out, b_w_in, b_lambda_re, b_lambda_im, b_log_step, b_b_re, b_b_im, b_c_re, b_c_im,
           b_d_skip, b_w_out, c_w_in, c_w_out, final_norm):
    bsz, seq, d = x.shape
    depth = norm_mix.shape[0]
    outs = []
    for bi in range(bsz):
        h = x[bi]
        for i in range(depth):
            kind, j = i % 3, i // 3
            if kind == 0:
                h = gmlp_layer(h, norm_mix[i], a_w_in[j], a_v_gain[j], a_w_spatial[j], a_b_spatial[j],
                               a_w_out[j])
            elif kind == 1:
                h = s5_layer(h, norm_mix[i], b_w_in[j], b_lambda_re[j], b_lambda_im[j], b_log_step[j],
                             b_b_re[j], b_b_im[j], b_c_re[j], b_c_im[j], b_d_skip[j], b_w_out[j])
            else:
                h = attn_layer(h, norm_mix[i], c_w_in[j], c_w_out[j])
            h = ffn_layer(h, norm_ffn[i], w_gate[i], w_up[i], w_down[i])
        outs.append(rmsnorm(h, final_norm, x.dtype))
    return jnp.stack(outs)
```

```python
import functools
import math

import jax
import jax.numpy as jnp
from jax import lax
from jax.experimental import pallas as pl
from jax.experimental.pallas import tpu as pltpu

F32 = jnp.float32
BF16 = jnp.bfloat16

RMS_EPS = 1e-6
LN_EPS = 1e-5

LANES = 128
SUBLANES = 8
VMEM_LIMIT_BYTES = 56 * 1024 * 1024

A_CHUNK = 128
A_GROUPS = 8
B_GROUP_CH = 16
S5_GROUPS_PER_BLOCK = LANES // B_GROUP_CH
S5_SEGMENTS = SUBLANES
S5_CHUNK_STEPS = 128
C_HEAD_DIM = 128
C_QBLOCK = 128
C_PATTERNS = ((128, 1), (512, 4), (2048, 16))


def _cparams(sem):
    return pltpu.CompilerParams(dimension_semantics=sem, vmem_limit_bytes=VMEM_LIMIT_BYTES)


def _rmsnorm_body(x_ref, g_ref, o_ref):
    x = x_ref[...]
    y = x * lax.rsqrt(jnp.mean(x * x, axis=-1, keepdims=True) + RMS_EPS)
    o_ref[...] = (y * g_ref[...]).astype(o_ref.dtype)


def rmsnorm(x, gain, out_dtype, tm=256):
    t, d = x.shape
    return pl.pallas_call(
        _rmsnorm_body,
        grid=(t // tm,),
        in_specs=[pl.BlockSpec((tm, d), lambda i: (i, 0)),
                  pl.BlockSpec((1, d), lambda i: (0, 0))],
        out_specs=pl.BlockSpec((tm, d), lambda i: (i, 0)),
        out_shape=jax.ShapeDtypeStruct((t, d), out_dtype),
        compiler_params=_cparams(("parallel",)),
        name="rmsnorm",
    )(x, gain.reshape(1, d))


def _mm_ws_body(*refs, n_b, has_res, nj, epilogue):
    a_ref = refs[0]
    w_refs = refs[1:1 + n_b]
    pos = 1 + n_b
    res_ref = refs[pos] if has_res else None
    pos += int(has_res)
    o_ref = refs[pos]
    wbf_refs = refs[pos + 1:]
    jj = pl.program_id(0)
    i = pl.program_id(1)
    kc = w_refs[0].shape[0]

    @pl.when(jj < nj)
    def _():
        slot = jj % 2
        rows = pl.ds(pl.multiple_of(i * kc, kc), kc)
        for w_ref, wbf in zip(w_refs, wbf_refs):
            wbf[slot, rows, :] = w_ref[...].astype(BF16)

    @pl.when(jj > 0)
    def _():
        slot = (jj - 1) % 2
        a = a_ref[...]
        accs = [jnp.dot(a, wbf[slot], preferred_element_type=F32) for wbf in wbf_refs]
        res = res_ref[...] if has_res else None
        o_ref[...] = epilogue(accs, res).astype(o_ref.dtype)


def matmul_ws(a, ws, layer, col_offsets, n_out, epilogue, out_dtype, *, res=None, tm, tn, name):
    m, kdim = a.shape
    ni = m // tm
    nj = pl.cdiv(n_out, tn)
    kc = kdim // ni
    assert m % tm == 0 and kdim % ni == 0 and kc % SUBLANES == 0
    assert all(off % tn == 0 for off in col_offsets)
    row = lambda jj, i: jnp.where(jj == 0, 0, i)
    in_specs = [pl.BlockSpec((tm, kdim), lambda jj, i: (row(jj, i), 0))]
    for off in col_offsets:
        in_specs.append(pl.BlockSpec(
            (None, kc, tn), lambda jj, i, o=off // tn: (layer, i, jnp.minimum(jj, nj - 1) + o)))
    args = [a] + list(ws)
    out_spec = pl.BlockSpec((tm, tn), lambda jj, i: (row(jj, i), jnp.maximum(jj - 1, 0)))
    if res is not None:
        in_specs.append(out_spec)
        args.append(res)
    return pl.pallas_call(
        functools.partial(_mm_ws_body, n_b=len(ws), has_res=res is not None, nj=nj, epilogue=epilogue),
        grid=(nj + 1, ni),
        in_specs=in_specs,
        out_specs=out_spec,
        out_shape=jax.ShapeDtypeStruct((m, n_out), out_dtype),
        scratch_shapes=[pltpu.VMEM((2, kdim, tn), BF16) for _ in ws],
        compiler_params=_cparams(("arbitrary", "arbitrary")),
        name=name,
    )(*args)


def _mm_rowres_body(a_ref, b_ref, res_ref, o_ref, acc_ref, *, nk):
    k = pl.program_id(1)
    j = pl.program_id(2)
    part = jnp.dot(a_ref[...], b_ref[...], preferred_element_type=F32)

    @pl.when(k == 0)
    def _():
        acc_ref[j] = part

    @pl.when((k > 0) & (k < nk - 1))
    def _():
        acc_ref[j] += part

    @pl.when(k == nk - 1)
    def _():
        o_ref[...] = res_ref[...] + acc_ref[j] + part


def matmul_rowres(a, b, layer, res, *, tm, tn, tk, name):
    m, kdim = a.shape
    n = b.shape[2]
    nk = kdim // tk
    assert m % tm == 0 and n % tn == 0 and kdim % tk == 0 and nk >= 2
    last = lambda i, k, j: (i, jnp.where(k == nk - 1, j, 0))
    return pl.pallas_call(
        functools.partial(_mm_rowres_body, nk=nk),
        grid=(m // tm, nk, n // tn),
        in_specs=[pl.BlockSpec((tm, tk), lambda i, k, j: (i, k)),
                  pl.BlockSpec((None, tk, tn), lambda i, k, j: (layer, k, j)),
                  pl.BlockSpec((tm, tn), last)],
        out_specs=pl.BlockSpec((tm, tn), last),
        out_shape=jax.ShapeDtypeStruct((m, n), F32),
        scratch_shapes=[pltpu.VMEM((n // tn, tm, tn), F32)],
        compiler_params=_cparams(("parallel", "arbitrary", "arbitrary")),
        name=name,
    )(a, b, res)


def _epi_id(accs, res):
    return accs[0]


def _epi_gelu(accs, res):
    return jax.nn.gelu(accs[0])


def _epi_res(accs, res):
    return res + accs[0]


def _epi_swiglu(accs, res):
    return jax.nn.silu(accs[0]) * accs[1]


def _epi_glu_res(accs, res):
    return res + accs[0] * jax.nn.sigmoid(accs[1])


def _gmlp_gate_body(u_ref, v_ref, gain_ref, ws_ref, bs_ref, o_ref, *, n_chunks, group_dim):
    v = v_ref[...].astype(F32)
    v = v - jnp.mean(v, axis=-1, keepdims=True)
    v = v * lax.rsqrt(jnp.mean(v * v, axis=-1, keepdims=True) + LN_EPS)
    vn = (v * gain_ref[...]).astype(BF16)
    row = lax.broadcasted_iota(jnp.int32, (A_CHUNK, A_CHUNK), 0)
    col = lax.broadcasted_iota(jnp.int32, (A_CHUNK, A_CHUNK), 1)
    causal = col <= row
    n_groups = ws_ref.shape[0]
    for g in range(n_groups):
        w = jnp.where(causal, ws_ref[g], 0.0).astype(BF16)
        bias = bs_ref[:, g:g + 1]
        cols = slice(g * group_dim, (g + 1) * group_dim)
        for c in range(n_chunks):
            rows = slice(c * A_CHUNK, (c + 1) * A_CHUNK)
            mixed = jnp.dot(w, vn[rows, cols], preferred_element_type=F32) + bias
            o_ref[rows, cols] = (u_ref[rows, cols].astype(F32) * mixed).astype(o_ref.dtype)


def gmlp_gate(uv, v_gain, w_spatial, b_spatial, n_chunks=2):
    t, w2 = uv.shape
    width = w2 // 2
    tm = n_chunks * A_CHUNK
    n_groups = w_spatial.shape[0]
    return pl.pallas_call(
        functools.partial(_gmlp_gate_body, n_chunks=n_chunks, group_dim=width // n_groups),
        grid=(t // tm,),
        in_specs=[pl.BlockSpec((tm, width), lambda i: (i, 0)),
                  pl.BlockSpec((tm, width), lambda i: (i, 1)),
                  pl.BlockSpec((1, width), lambda i: (0, 0)),
                  pl.BlockSpec((n_groups, A_CHUNK, A_CHUNK), lambda i: (0, 0, 0)),
                  pl.BlockSpec((A_CHUNK, n_groups), lambda i: (0, 0))],
        out_specs=pl.BlockSpec((tm, width), lambda i: (i, 0)),
        out_shape=jax.ShapeDtypeStruct((t, width), BF16),
        compiler_params=_cparams(("parallel",)),
        name="gmlp_gate",
    )(uv, uv, v_gain.reshape(1, width), w_spatial, b_spatial.T)


def _s5_discretize_body(lr_ref, li_ref, ls_ref, br_ref, bi_ref, abr_ref, abi_ref, bbr_ref, bbi_ref):
    lr, li = lr_ref[...], li_ref[...]
    dt = jnp.exp(ls_ref[...])
    decay = jnp.exp(lr * dt)
    ab_re, ab_im = decay * jnp.cos(li * dt), decay * jnp.sin(li * dt)
    den = lr * lr + li * li
    nr, ni = ab_re - 1.0, ab_im
    coef_re = (nr * lr + ni * li) / den
    coef_im = (ni * lr - nr * li) / den
    br, bi = br_ref[...], bi_ref[...]
    abr_ref[...] = ab_re
    abi_ref[...] = ab_im
    bbr_ref[...] = coef_re * br - coef_im * bi
    bbi_ref[...] = coef_re * bi + coef_im * br


def s5_discretize(lam_re, lam_im, log_step, b_re, b_im):
    g, p = lam_re.shape
    rows = g * B_GROUP_CH
    rep = lambda z: jnp.repeat(z, B_GROUP_CH, axis=0)
    b_rows = lambda b: jnp.transpose(b, (0, 2, 1)).reshape(rows, p)
    out = jax.ShapeDtypeStruct((rows, p), F32)
    return pl.pallas_call(
        _s5_discretize_body,
        out_shape=(out, out, out, out),
        name="s5_discretize",
    )(rep(lam_re), rep(lam_im), rep(log_step.reshape(g, 1)), b_rows(b_re), b_rows(b_im))


def _s5_scan_body(u_ref, bblk_ref, cre_ref, cim_ref, ar_ref, ai_ref, d_ref, y_ref,
                  lhs_ref, bu_ref, x_ref, yl_ref, *, steps, n_state):
    ls = S5_CHUNK_STEPS
    n_chunks = steps // ls
    ar = jnp.broadcast_to(ar_ref[...], (S5_SEGMENTS, n_state))
    ai = jnp.broadcast_to(ai_ref[...], (S5_SEGMENTS, n_state))

    def project(c):
        def gather(i, carry):
            lhs_ref[pl.ds(i * S5_SEGMENTS, S5_SEGMENTS), :] = (
                u_ref[pl.ds(c * ls + i, S5_SEGMENTS, stride=steps), :])
            return carry
        lax.fori_loop(0, ls, gather, 0, unroll=8)
        bu_ref[...] = jnp.dot(lhs_ref[...].astype(BF16), bblk_ref[...], preferred_element_type=F32)

    def recur(state, store):
        def step(i, carry):
            xr, xi = carry
            rows = pl.ds(i * S5_SEGMENTS, S5_SEGMENTS)
            nxr = ar * xr - ai * xi + bu_ref[rows, 0:n_state]
            nxi = ar * xi + ai * xr + bu_ref[rows, n_state:2 * n_state]
            if store:
                x_ref[rows, 0:n_state] = nxr
                x_ref[rows, n_state:2 * n_state] = nxi
            return nxr, nxi
        return lax.fori_loop(0, ls, step, state, unroll=2)

    zero = jnp.zeros((S5_SEGMENTS, n_state), F32)

    def pass1(c, state):
        project(c)
        return recur(state, False)
    er, ei = lax.fori_loop(0, n_chunks, pass1, (zero, zero))

    pr, pi = ar_ref[...], ai_ref[...]
    for _ in range(int(math.log2(steps))):
        pr, pi = pr * pr - pi * pi, 2.0 * pr * pi
    cr = jnp.zeros((1, n_state), F32)
    ci = jnp.zeros((1, n_state), F32)
    init_r, init_i = [], []
    for j in range(S5_SEGMENTS):
        init_r.append(cr)
        init_i.append(ci)
        cr, ci = (er[j:j + 1] + pr * cr - pi * ci, ei[j:j + 1] + pr * ci + pi * cr)
    x0 = (jnp.concatenate(init_r, axis=0), jnp.concatenate(init_i, axis=0))

    d = d_ref[...]

    def pass2(c, state):
        project(c)
        state = recur(state, True)
        xr = x_ref[:, 0:n_state].astype(BF16)
        xi = x_ref[:, n_state:2 * n_state].astype(BF16)
        y = (jnp.dot(xr, cre_ref[...], preferred_element_type=F32)
             - jnp.dot(xi, cim_ref[...], preferred_element_type=F32))
        yl_ref[...] = jax.nn.gelu(y + d * lhs_ref[...])

        def emit(i, carry):
            y_ref[pl.ds(c * ls + i, S5_SEGMENTS, stride=steps), :] = (
                yl_ref[pl.ds(i * S5_SEGMENTS, S5_SEGMENTS), :])
            return carry
        lax.fori_loop(0, ls, emit, 0, unroll=8)
        return state
    lax.fori_loop(0, n_chunks, pass2, x0)


def s5_scan(u, ab_re, ab_im, bb_re, bb_im, c_re, c_im, d_skip):
    t, width = u.shape
    g, p = ab_re.shape
    gb = S5_GROUPS_PER_BLOCK
    n_blocks = g // gb
    n_state = gb * p
    steps = t // S5_SEGMENTS
    assert steps % S5_CHUNK_STEPS == 0 and steps & (steps - 1) == 0
    eye = jnp.eye(gb, dtype=F32)

    def blockdiag_in(bb):
        z = bb.reshape(n_blocks, gb, B_GROUP_CH, p)
        return jnp.einsum('bgcp,gh->bgchp', z, eye).reshape(n_blocks, gb * B_GROUP_CH, n_state)

    def blockdiag_out(cc):
        z = cc.reshape(n_blocks, gb, B_GROUP_CH, p)
        return jnp.einsum('bgcp,gh->bhpgc', z, eye).reshape(n_blocks, n_state, gb * B_GROUP_CH)

    bblk = jnp.concatenate([blockdiag_in(bb_re), blockdiag_in(bb_im)], axis=-1).astype(BF16)
    cre = blockdiag_out(c_re).astype(BF16)
    cim = blockdiag_out(c_im).astype(BF16)
    ar = ab_re.reshape(n_blocks, 1, n_state)
    ai = ab_im.reshape(n_blocks, 1, n_state)
    d = d_skip.reshape(n_blocks, 1, LANES)
    rows_chunk = S5_CHUNK_STEPS * S5_SEGMENTS
    return pl.pallas_call(
        functools.partial(_s5_scan_body, steps=steps, n_state=n_state),
        grid=(n_blocks,),
        in_specs=[pl.BlockSpec((t, LANES), lambda b: (0, b)),
                  pl.BlockSpec((None, LANES, 2 * n_state), lambda b: (b, 0, 0)),
                  pl.BlockSpec((None, n_state, LANES), lambda b: (b, 0, 0)),
                  pl.BlockSpec((None, n_state, LANES), lambda b: (b, 0, 0)),
                  pl.BlockSpec((None, 1, n_state), lambda b: (b, 0, 0)),
                  pl.BlockSpec((None, 1, n_state), lambda b: (b, 0, 0)),
                  pl.BlockSpec((None, 1, LANES), lambda b: (b, 0, 0))],
        out_specs=pl.BlockSpec((t, LANES), lambda b: (0, b)),
        out_shape=jax.ShapeDtypeStruct((t, width), F32),
        scratch_shapes=[pltpu.VMEM((rows_chunk, LANES), F32),
                        pltpu.VMEM((rows_chunk, 2 * n_state), F32),
                        pltpu.VMEM((rows_chunk, 2 * n_state), F32),
                        pltpu.VMEM((rows_chunk, LANES), F32)],
        compiler_params=_cparams(("parallel",)),
        name="s5_scan",
    )(u, bblk, cre, cim, ar, ai, d)


def _attn_body(q_ref, kp_ref, kc_ref, vp_ref, vc_ref, o_ref, lse_ref, *, blocks_per_residue, n_heads):
    b = pl.program_id(0)
    not_first = (b % blocks_per_residue) != 0
    qi = lax.broadcasted_iota(jnp.int32, (C_QBLOCK, 2 * C_QBLOCK), 0)
    kj = lax.broadcasted_iota(jnp.int32, (C_QBLOCK, 2 * C_QBLOCK), 1)
    dist = C_QBLOCK + qi - kj
    valid = (dist >= 0) & (dist <= C_QBLOCK) & (not_first | (kj >= C_QBLOCK))
    scale = C_HEAD_DIM ** -0.5
    for h in range(n_heads):
        cols = slice(h * C_HEAD_DIM, (h + 1) * C_HEAD_DIM)
        q = q_ref[:, cols]
        k = jnp.concatenate([kp_ref[:, cols], kc_ref[:, cols]], axis=0)
        v = jnp.concatenate([vp_ref[:, cols], vc_ref[:, cols]], axis=0)
        s = lax.dot_general(q, k, (((1,), (1,)), ((), ())), preferred_element_type=F32) * scale
        s = jnp.where(valid, s, -jnp.inf)
        m = jnp.max(s, axis=-1, keepdims=True)
        e = jnp.exp(s - m)
        ssum = jnp.sum(e, axis=-1, keepdims=True)
        p = e / ssum
        o_ref[:, cols] = jnp.dot(p.astype(BF16), v, preferred_element_type=F32).astype(o_ref.dtype)
        lse_ref[:, h:h + 1] = m + jnp.log(ssum)


def window_attention(qkv, dil):
    t, w3 = qkv.shape
    width = w3 // 3
    n_heads = width // C_HEAD_DIM
    n_blocks = t // C_QBLOCK
    cur = lambda part: pl.BlockSpec((C_QBLOCK, width), lambda b, p=part: (b, p))
    prev = lambda part: pl.BlockSpec((C_QBLOCK, width), lambda b, p=part: (jnp.maximum(b - 1, 0), p))
    return pl.pallas_call(
        functools.partial(_attn_body, blocks_per_residue=n_blocks // dil, n_heads=n_heads),
        grid=(n_blocks,),
        in_specs=[cur(0), prev(1), cur(1), prev(2), cur(2)],
        out_specs=(pl.BlockSpec((C_QBLOCK, width), lambda b: (b, 0)),
                   pl.BlockSpec((C_QBLOCK, n_heads), lambda b: (b, 0))),
        out_shape=(jax.ShapeDtypeStruct((t, width), BF16),
                   jax.ShapeDtypeStruct((t, n_heads), F32)),
        compiler_params=_cparams(("parallel",)),
        name=f"window_attention_d{dil}",
    )(qkv, qkv, qkv, qkv, qkv)


def _residue_permutation(n, dil, inverse=False):
    row = lax.broadcasted_iota(jnp.int32, (n, n), 0)
    col = lax.broadcasted_iota(jnp.int32, (n, n), 1)
    per = n // dil
    if inverse:
        hit = col == (row % dil) * per + row // dil
    else:
        hit = col == (row % per) * dil + row // per
    return jnp.where(hit, 1.0, 0.0).astype(BF16)


def _rmsnorm_residue_body(x_ref, g_ref, *o_refs, dils):
    x = x_ref[...]
    y = x * lax.rsqrt(jnp.mean(x * x, axis=-1, keepdims=True) + RMS_EPS)
    yb = (y * g_ref[...]).astype(BF16)
    tm = x.shape[0]
    for dil, o_ref in zip(dils, o_refs):
        if dil == 1:
            o_ref[0] = yb
            continue
        per = tm // dil
        shuffled = jnp.dot(_residue_permutation(tm, dil), yb, preferred_element_type=F32).astype(BF16)
        for r in range(dil):
            o_ref[r] = shuffled[r * per:(r + 1) * per]


def rmsnorm_residue_major(x, gain, dils, tm=256):
    t, d = x.shape
    outs = pl.pallas_call(
        functools.partial(_rmsnorm_residue_body, dils=dils),
        grid=(t // tm,),
        in_specs=[pl.BlockSpec((tm, d), lambda i: (i, 0)),
                  pl.BlockSpec((1, d), lambda i: (0, 0))],
        out_specs=tuple(pl.BlockSpec((dil, tm // dil, d), lambda i: (0, i, 0)) for dil in dils),
        out_shape=tuple(jax.ShapeDtypeStruct((dil, t // dil, d), BF16) for dil in dils),
        compiler_params=_cparams(("parallel",)),
        name="rmsnorm_residue_major",
    )(x, gain.reshape(1, d))
    return [o.reshape(t, d) for o in outs]


def _attn_combine_body(*refs, dils):
    n = len(dils)
    o_refs, l_refs, out_ref = refs[:n], refs[n:2 * n], refs[2 * n]
    scr_refs = refs[2 * n + 1:]
    tm, width = out_ref.shape
    nat = []
    scr = iter(scr_refs)
    for dil, o_ref in zip(dils, o_refs):
        if dil == 1:
            nat.append(o_ref.at[0])
            continue
        s_ref = next(scr)
        o2d = jnp.concatenate([o_ref[r] for r in range(dil)], axis=0)
        s_ref[...] = jnp.dot(_residue_permutation(tm, dil, inverse=True), o2d,
                             preferred_element_type=F32)
        nat.append(s_ref)
    ls = [l[...] for l in l_refs]
    mx = functools.reduce(jnp.maximum, ls)
    es = [jnp.exp(l - mx) for l in ls]
    den = functools.reduce(lambda a, b: a + b, es)
    alphas = [e / den for e in es]
    n_heads = ls[0].shape[1]
    for h in range(n_heads):
        cols = slice(h * C_HEAD_DIM, (h + 1) * C_HEAD_DIM)
        acc = None
        for alpha, o in zip(alphas, nat):
            term = alpha[:, h:h + 1] * o[:, cols].astype(F32)
            acc = term if acc is None else acc + term
        out_ref[:, cols] = acc.astype(out_ref.dtype)


def attn_combine(outs, lses, dils, tm=256):
    t, width = outs[0].shape
    n_heads = lses[0].shape[1]
    o_specs = [pl.BlockSpec((dil, tm // dil, width), lambda i: (0, i, 0)) for dil in dils]
    l_spec = pl.BlockSpec((tm, n_heads), lambda i: (i, 0))
    return pl.pallas_call(
        functools.partial(_attn_combine_body, dils=dils),
        grid=(t // tm,),
        in_specs=o_specs + [l_spec] * len(dils),
        out_specs=pl.BlockSpec((tm, width), lambda i: (i, 0)),
        out_shape=jax.ShapeDtypeStruct((t, width), BF16),
        scratch_shapes=[pltpu.VMEM((tm, width), F32) for dil in dils if dil > 1],
        compiler_params=_cparams(("parallel",)),
        name="attn_combine",
    )(*[o.reshape(dil, t // dil, width) for o, dil in zip(outs, dils)], *lses)


def _from_residue_major(a, dil):
    t, d = a.shape
    return a.reshape(dil, t // dil, d).transpose(1, 0, 2).reshape(t, d)


def ffn_layer(h, gain, w_gate, w_up, w_down_bf16, layer, tm=1024):
    ff = w_gate.shape[2]
    hn = rmsnorm(h, gain, BF16)
    mid = matmul_ws(hn, [w_gate, w_up], layer, [0, 0], ff, _epi_swiglu, BF16, tm=tm, tn=512,
                    name="ffn_gate_up")
    return matmul_rowres(mid, w_down_bf16, layer, h, tm=tm, tn=256, tk=ff // 2, name="ffn_down")


def gmlp_layer(h, gain, w_in, v_gain, w_spatial, b_spatial, w_out, layer):
    d = h.shape[1]
    hn = rmsnorm(h, gain, BF16)
    uv = matmul_ws(hn, [w_in], layer, [0], w_in.shape[2], _epi_gelu, BF16, tm=1024, tn=1024,
                   name="gmlp_in")
    gated = gmlp_gate(uv, v_gain, w_spatial, b_spatial)
    return matmul_ws(gated, [w_out], layer, [0], d, _epi_res, F32, res=h, tm=512, tn=1024,
                     name="gmlp_out")


def s5_layer(h, gain, w_in, lam_re, lam_im, log_step, b_re, b_im, c_re, c_im, d_skip, w_out, layer):
    d = h.shape[1]
    hn = rmsnorm(h, gain, BF16)
    u = matmul_ws(hn, [w_in], layer, [0], w_in.shape[2], _epi_id, F32, tm=1024,
                  tn=min(1024, w_in.shape[2]), name="s5_in")
    ab_re, ab_im, bb_re, bb_im = s5_discretize(lam_re, lam_im, log_step, b_re, b_im)
    y = s5_scan(u, ab_re[::B_GROUP_CH], ab_im[::B_GROUP_CH], bb_re, bb_im, c_re, c_im, d_skip)
    return matmul_ws(y.astype(BF16), [w_out, w_out], layer, [0, d], d, _epi_glu_res, F32, res=h,
                     tm=1024, tn=1024, name="s5_out")


def attn_layer(h, gain, w_in, w_out, layer):
    d = h.shape[1]
    width = w_out.shape[1]
    dils = tuple(dil for _, dil in C_PATTERNS)
    assert all(window // dil == C_QBLOCK for window, dil in C_PATTERNS)
    hns = rmsnorm_residue_major(h, gain, dils)
    tn = 1024 if (3 * width) % 1024 == 0 else 512
    outs, lses = [], []
    for p_idx, dil in enumerate(dils):
        qkv = matmul_ws(hns[p_idx], [w_in], layer, [p_idx * 3 * width], 3 * width, _epi_id, BF16,
                        tm=1024, tn=tn, name=f"attn_in_d{dil}")
        o, lse = window_attention(qkv, dil)
        outs.append(o)
        lses.append(_from_residue_major(lse, dil))
    mixed = attn_combine(outs, lses, dils)
    return matmul_ws(mixed, [w_out], layer, [0], d, _epi_res, F32, res=h, tm=1024, tn=1024,
                     name="attn_out")


def kernel(x, norm_mix, norm_ffn, w_gate, w_up, w_down, a_w_in, a_v_gain, a_w_spatial, a_b_spatial,
           a_w_out, b_w_in, b_lambda_re, b_lambda_im, b_log_step, b_b_re, b_b_im, b_c_re, b_c_im,
           b_d_skip, b_w_out, c_w_in, c_w_out, final_norm):
    bsz, seq, d = x.shape
    depth = norm_mix.shape[0]
    w_down_bf16 = w_down.astype(BF16)
    outs = []
    for bi in range(bsz):
        h = x[bi]
        for i in range(depth):
            kind, j = i % 3, i // 3
            if kind == 0:
                h = gmlp_layer(h, norm_mix[i], a_w_in, a_v_gain[j], a_w_spatial[j], a_b_spatial[j],
                               a_w_out, j)
            elif kind == 1:
                h = s5_layer(h, norm_mix[i], b_w_in, b_lambda_re[j], b_lambda_im[j], b_log_step[j],
                             b_b_re[j], b_b_im[j], b_c_re[j], b_c_im[j], b_d_skip[j], b_w_out, j)
            else:
                h = attn_layer(h, norm_mix[i], c_w_in, c_w_out, j)
            h = ffn_layer(h, norm_ffn[i], w_gate, w_up, w_down_bf16, i)
        outs.append(rmsnorm(h, final_norm, x.dtype))
    return jnp.stack(outs)
```

```python
import functools
import math

import jax
import jax.numpy as jnp
from jax import lax
from jax.experimental import pallas as pl
from jax.experimental.pallas import tpu as pltpu

F32 = jnp.float32
BF16 = jnp.bfloat16

RMS_EPS = 1e-6
LN_EPS = 1e-5

LANES = 128
SUBLANES = 8
BF16_SUBLANES = 16
VMEM_CAPACITY_BYTES = 64 * 1024 * 1024
VMEM_RESERVE_BYTES = 3 * 1024 * 1024
VMEM_LIMIT_BYTES = 56 * 1024 * 1024

A_CHUNK = 128
A_GROUPS = 8
B_GROUP_CH = 16
S5_GROUPS_PER_BLOCK = LANES // B_GROUP_CH
S5_SEGMENTS = SUBLANES
S5_CHUNK_STEPS = 128
C_HEAD_DIM = 128
C_QBLOCK = 128
C_PATTERNS = ((128, 1), (512, 4), (2048, 16))


def _cparams(sem, vmem_limit_bytes=VMEM_LIMIT_BYTES):
    return pltpu.CompilerParams(dimension_semantics=sem, vmem_limit_bytes=vmem_limit_bytes)


def _rmsnorm_body(x_ref, g_ref, o_ref):
    x = x_ref[...]
    y = x * lax.rsqrt(jnp.mean(x * x, axis=-1, keepdims=True) + RMS_EPS)
    o_ref[...] = (y * g_ref[...]).astype(o_ref.dtype)


def rmsnorm(x, gain, out_dtype, tm=256):
    t, d = x.shape
    return pl.pallas_call(
        _rmsnorm_body,
        grid=(t // tm,),
        in_specs=[pl.BlockSpec((tm, d), lambda i: (i, 0)),
                  pl.BlockSpec((1, d), lambda i: (0, 0))],
        out_specs=pl.BlockSpec((tm, d), lambda i: (i, 0)),
        out_shape=jax.ShapeDtypeStruct((t, d), out_dtype),
        compiler_params=_cparams(("parallel",)),
        name="rmsnorm",
    )(x, gain.reshape(1, d))


def _lane_group_sum(x):
    out = x[:, 0:LANES]
    for g in range(1, x.shape[1] // LANES):
        out = out + x[:, g * LANES:(g + 1) * LANES]
    return out


def _rms_row_scale(ssq_ref, d_model):
    return lax.rsqrt(jnp.sum(ssq_ref[...], axis=-1, keepdims=True) / d_model + RMS_EPS)


def _stream_stats_body(x_ref, hb_ref, ssq_ref):
    x = x_ref[...]
    hb_ref[...] = x.astype(BF16)
    ssq_ref[...] = _lane_group_sum(x * x)


def stream_stats(x, tm=256):
    t, d = x.shape
    return pl.pallas_call(
        _stream_stats_body,
        grid=(t // tm,),
        in_specs=[pl.BlockSpec((tm, d), lambda i: (i, 0))],
        out_specs=(pl.BlockSpec((tm, d), lambda i: (i, 0)),
                   pl.BlockSpec((tm, LANES), lambda i: (i, 0))),
        out_shape=(jax.ShapeDtypeStruct((t, d), BF16), jax.ShapeDtypeStruct((t, LANES), F32)),
        compiler_params=_cparams(("parallel",)),
        name="stream_stats",
    )(x)


def _mm_ws_body(*refs, n_b, has_res, has_norm, emit_stats, has_side, nj, sub, d_model, epilogue):
    it = iter(refs)
    a_ref = next(it)
    w_refs = [next(it) for _ in range(n_b)]
    res_ref = next(it) if has_res else None
    ssq_ref, gain_ref = (next(it), next(it)) if has_norm else (None, None)
    side_ref = next(it) if has_side else None
    o_ref = next(it)
    hb_ref, ssq_out_ref = (next(it), next(it)) if emit_stats else (None, None)
    side_out_ref = next(it) if has_side else None
    wbf_refs = [(next(it), next(it)) for _ in range(n_b)]
    ssq_acc_ref = next(it) if emit_stats else None
    jj = pl.program_id(0)
    i = pl.program_id(1)
    kc = w_refs[0].shape[0]

    def cast_into(buf):
        rows = pl.ds(pl.multiple_of(i * kc, kc), kc)
        for w_ref, wbf in zip(w_refs, wbf_refs):
            w = w_ref[...]
            if has_norm:
                w = w * gain_ref[...]
            wbf[buf][rows, :] = w.astype(BF16)

    def side_cast():
        if has_side:
            side_out_ref[...] = side_ref[...].astype(BF16)

    def compute_from(buf):
        a = a_ref[...]
        rs = _rms_row_scale(ssq_ref, d_model) if has_norm else None
        total = None
        for c in range(o_ref.shape[1] // sub):
            cols = slice(c * sub, (c + 1) * sub)
            accs = [jnp.dot(a, wbf[buf][:, cols], preferred_element_type=F32) for wbf in wbf_refs]
            if has_norm:
                accs = [acc * rs for acc in accs]
            out = epilogue(accs, res_ref[:, cols] if has_res else None)
            o_ref[:, cols] = out.astype(o_ref.dtype)
            if emit_stats:
                hb_ref[:, cols] = out.astype(BF16)
                part = _lane_group_sum(out * out)
                total = part if total is None else total + part
        return total

    def add_stats(total, first):
        if emit_stats:
            ssq_acc_ref[i] = total if first else ssq_acc_ref[i] + total
            ssq_out_ref[...] = ssq_acc_ref[i]

    @pl.when(jj == 0)
    def _():
        cast_into(0)
        side_cast()

    if nj > 1:
        @pl.when(jj == 1)
        def _():
            total = compute_from(0)
            cast_into(1)
            side_cast()
            add_stats(total, True)

    for parity in (0, 1):
        @pl.when((jj > 1) & (jj < nj) & (jj % 2 == parity))
        def _():
            total = compute_from(1 - parity)
            cast_into(parity)
            side_cast()
            add_stats(total, False)

    @pl.when(jj == nj)
    def _():
        total = compute_from((nj - 1) % 2)
        side_cast()
        add_stats(total, nj == 1)


def matmul_ws(a, ws, layer, col_offsets, n_out, epilogue, out_dtype, *, res=None, norm=None,
              emit_stats=False, side=None, tm, tn, sub=256, name):
    m, kdim = a.shape
    ni = m // tm
    nj = pl.cdiv(n_out, tn)
    kc = kdim // ni
    assert m % tm == 0 and kdim % ni == 0 and kc % SUBLANES == 0
    assert all(off % tn == 0 for off in col_offsets)
    assert not emit_stats or (res is not None and out_dtype == F32 and n_out % tn == 0)
    row = lambda jj, i: jnp.where(jj == 0, 0, i)
    in_specs = [pl.BlockSpec((tm, kdim), lambda jj, i: (row(jj, i), 0))]
    for off in col_offsets:
        in_specs.append(pl.BlockSpec(
            (None, kc, tn), lambda jj, i, o=off // tn: (layer, i, jnp.minimum(jj, nj - 1) + o)))
    args = [a] + list(ws)
    out_spec = pl.BlockSpec((tm, tn), lambda jj, i: (row(jj, i), jnp.maximum(jj - 1, 0)))
    stat_spec = pl.BlockSpec((tm, LANES), lambda jj, i: (row(jj, i), 0))
    if res is not None:
        in_specs.append(out_spec)
        args.append(res)
    if norm is not None:
        ssq, gain = norm
        in_specs += [stat_spec, pl.BlockSpec((kc, 1), lambda jj, i: (i, 0))]
        args += [ssq, gain.reshape(kdim, 1)]
    out_specs = [out_spec]
    out_shape = [jax.ShapeDtypeStruct((m, n_out), out_dtype)]
    scratch = [pltpu.VMEM((kdim, tn), BF16) for _ in ws for _ in range(2)]
    if emit_stats:
        out_specs += [out_spec, stat_spec]
        out_shape += [jax.ShapeDtypeStruct((m, n_out), BF16), jax.ShapeDtypeStruct((m, LANES), F32)]
        scratch.append(pltpu.VMEM((ni, tm, LANES), F32))
    if side is not None:
        stack, side_layer = side
        _, s_rows, s_cols = stack.shape
        sr = next(r for r in range(BF16_SUBLANES, s_rows + 1, BF16_SUBLANES)
                  if s_rows % r == 0 and s_rows // r <= (nj + 1) * ni)
        n_side = s_rows // sr
        blk = lambda jj, i: jnp.minimum(jj * ni + i, n_side - 1)
        in_specs.append(pl.BlockSpec((None, sr, s_cols), lambda jj, i: (side_layer, blk(jj, i), 0)))
        args.append(stack)
        out_specs.append(pl.BlockSpec((sr, s_cols), lambda jj, i: (blk(jj, i), 0)))
        out_shape.append(jax.ShapeDtypeStruct((s_rows, s_cols), BF16))
    stat_tile = tm * LANES * 4
    vmem_bytes = (2 * tm * kdim * 2 + len(ws) * (2 * kdim * tn * 2 + 2 * kc * tn * 4)
                  + 2 * tm * tn * jnp.dtype(out_dtype).itemsize + 2 * len(ws) * tm * sub * 4)
    if res is not None:
        vmem_bytes += 2 * tm * tn * 4
    if norm is not None:
        vmem_bytes += 2 * stat_tile + 2 * kc * LANES * 4
    if emit_stats:
        vmem_bytes += 2 * tm * tn * 2 + (2 + ni) * stat_tile
    if side is not None:
        vmem_bytes += 2 * sr * s_cols * (4 + 2)
    assert vmem_bytes + VMEM_RESERVE_BYTES <= VMEM_CAPACITY_BYTES
    outs = pl.pallas_call(
        functools.partial(_mm_ws_body, n_b=len(ws), has_res=res is not None, has_norm=norm is not None,
                          emit_stats=emit_stats, has_side=side is not None, nj=nj, sub=min(sub, tn),
                          d_model=kdim, epilogue=epilogue),
        grid=(nj + 1, ni),
        in_specs=in_specs,
        out_specs=tuple(out_specs),
        out_shape=tuple(out_shape),
        scratch_shapes=scratch,
        compiler_params=_cparams(("arbitrary", "arbitrary"),
                                 max(VMEM_LIMIT_BYTES, vmem_bytes + VMEM_RESERVE_BYTES)),
        name=name,
    )(*args)
    return outs[0] if len(outs) == 1 else outs


def _mm_rowres_body(*refs, nk, sub, emit_stats):
    if emit_stats:
        a_ref, b_ref, res_ref, o_ref, hb_ref, ssq_out_ref, acc_ref, ssq_acc_ref = refs
    else:
        a_ref, b_ref, res_ref, o_ref, acc_ref = refs
    k = pl.program_id(1)
    j = pl.program_id(2)

    @pl.when(k == 0)
    def _():
        acc_ref[j] = jnp.dot(a_ref[...], b_ref[...], preferred_element_type=F32)

    @pl.when((k > 0) & (k < nk - 1))
    def _():
        acc_ref[j] += jnp.dot(a_ref[...], b_ref[...], preferred_element_type=F32)

    def finish(first):
        a = a_ref[...]
        total = None
        for c in range(o_ref.shape[1] // sub):
            cols = slice(c * sub, (c + 1) * sub)
            out = (res_ref[:, cols] + acc_ref[j, :, cols]
                   + jnp.dot(a, b_ref[:, cols], preferred_element_type=F32))
            o_ref[:, cols] = out
            if emit_stats:
                hb_ref[:, cols] = out.astype(BF16)
                part = _lane_group_sum(out * out)
                total = part if total is None else total + part
        if emit_stats:
            ssq_acc_ref[...] = total if first else ssq_acc_ref[...] + total
            ssq_out_ref[...] = ssq_acc_ref[...]

    @pl.when((k == nk - 1) & (j == 0))
    def _():
        finish(True)

    @pl.when((k == nk - 1) & (j > 0))
    def _():
        finish(False)


def matmul_rowres(a, b, res, *, emit_stats, tm, tn, tk, name):
    m, kdim = a.shape
    n = b.shape[1]
    nk = kdim // tk
    assert m % tm == 0 and n % tn == 0 and kdim % tk == 0 and nk >= 2
    last = lambda i, k, j: (i, jnp.where(k == nk - 1, j, 0))
    tile_f32 = tm * tn * 4
    vmem_bytes = (n // tn + 4) * tile_f32 + 2 * (tm * tk + tk * tn) * 2
    out_specs = [pl.BlockSpec((tm, tn), last)]
    out_shape = [jax.ShapeDtypeStruct((m, n), F32)]
    scratch = [pltpu.VMEM((n // tn, tm, tn), F32)]
    if emit_stats:
        out_specs += [pl.BlockSpec((tm, tn), last), pl.BlockSpec((tm, LANES), lambda i, k, j: (i, 0))]
        out_shape += [jax.ShapeDtypeStruct((m, n), BF16), jax.ShapeDtypeStruct((m, LANES), F32)]
        scratch.append(pltpu.VMEM((tm, LANES), F32))
        vmem_bytes += 2 * tm * tn * 2 + 3 * tm * LANES * 4
    assert vmem_bytes + VMEM_RESERVE_BYTES <= VMEM_CAPACITY_BYTES
    outs = pl.pallas_call(
        functools.partial(_mm_rowres_body, nk=nk, sub=min(256, tn), emit_stats=emit_stats),
        grid=(m // tm, nk, n // tn),
        in_specs=[pl.BlockSpec((tm, tk), lambda i, k, j: (i, k)),
                  pl.BlockSpec((tk, tn), lambda i, k, j: (k, j)),
                  pl.BlockSpec((tm, tn), last, pipeline_mode=pl.Buffered(1))],
        out_specs=tuple(out_specs),
        out_shape=tuple(out_shape),
        scratch_shapes=scratch,
        compiler_params=_cparams(("parallel", "arbitrary", "arbitrary"),
                                 max(VMEM_LIMIT_BYTES, vmem_bytes + VMEM_RESERVE_BYTES)),
        name=name,
    )(a, b, res)
    return outs if emit_stats else outs[0]


def _epi_id(accs, res):
    return accs[0]


def _epi_gelu(accs, res):
    return jax.nn.gelu(accs[0])


def _epi_res(accs, res):
    return res + accs[0]


def _epi_swiglu(accs, res):
    return jax.nn.silu(accs[0]) * accs[1]


def _epi_glu_res(accs, res):
    return res + accs[0] * jax.nn.sigmoid(accs[1])


def _gmlp_gate_body(u_ref, v_ref, gain_ref, ws_ref, bs_ref, o_ref, *, n_chunks, group_dim):
    v = v_ref[...].astype(F32)
    v = v - jnp.mean(v, axis=-1, keepdims=True)
    v = v * lax.rsqrt(jnp.mean(v * v, axis=-1, keepdims=True) + LN_EPS)
    vn = (v * gain_ref[...]).astype(BF16)
    row = lax.broadcasted_iota(jnp.int32, (A_CHUNK, A_CHUNK), 0)
    col = lax.broadcasted_iota(jnp.int32, (A_CHUNK, A_CHUNK), 1)
    causal = col <= row
    n_groups = ws_ref.shape[0]
    for g in range(n_groups):
        w = jnp.where(causal, ws_ref[g], 0.0).astype(BF16)
        bias = bs_ref[:, g:g + 1]
        cols = slice(g * group_dim, (g + 1) * group_dim)
        for c in range(n_chunks):
            rows = slice(c * A_CHUNK, (c + 1) * A_CHUNK)
            mixed = jnp.dot(w, vn[rows, cols], preferred_element_type=F32) + bias
            o_ref[rows, cols] = (u_ref[rows, cols].astype(F32) * mixed).astype(o_ref.dtype)


def gmlp_gate(uv, v_gain, w_spatial, b_spatial, n_chunks=2):
    t, w2 = uv.shape
    width = w2 // 2
    tm = n_chunks * A_CHUNK
    n_groups = w_spatial.shape[0]
    return pl.pallas_call(
        functools.partial(_gmlp_gate_body, n_chunks=n_chunks, group_dim=width // n_groups),
        grid=(t // tm,),
        in_specs=[pl.BlockSpec((tm, width), lambda i: (i, 0)),
                  pl.BlockSpec((tm, width), lambda i: (i, 1)),
                  pl.BlockSpec((1, width), lambda i: (0, 0)),
                  pl.BlockSpec((n_groups, A_CHUNK, A_CHUNK), lambda i: (0, 0, 0)),
                  pl.BlockSpec((A_CHUNK, n_groups), lambda i: (0, 0))],
        out_specs=pl.BlockSpec((tm, width), lambda i: (i, 0)),
        out_shape=jax.ShapeDtypeStruct((t, width), BF16),
        compiler_params=_cparams(("parallel",)),
        name="gmlp_gate",
    )(uv, uv, v_gain.reshape(1, width), w_spatial, b_spatial.T)


def _s5_discretize_body(lr_ref, li_ref, ls_ref, br_ref, bi_ref, abr_ref, abi_ref, bbr_ref, bbi_ref):
    lr, li = lr_ref[...], li_ref[...]
    dt = jnp.exp(ls_ref[...])
    decay = jnp.exp(lr * dt)
    ab_re, ab_im = decay * jnp.cos(li * dt), decay * jnp.sin(li * dt)
    den = lr * lr + li * li
    nr, ni = ab_re - 1.0, ab_im
    coef_re = (nr * lr + ni * li) / den
    coef_im = (ni * lr - nr * li) / den
    br, bi = br_ref[...], bi_ref[...]
    abr_ref[...] = ab_re
    abi_ref[...] = ab_im
    bbr_ref[...] = coef_re * br - coef_im * bi
    bbi_ref[...] = coef_re * bi + coef_im * br


def s5_discretize(lam_re, lam_im, log_step, b_re, b_im):
    g, p = lam_re.shape
    rows = g * B_GROUP_CH
    rep = lambda z: jnp.repeat(z, B_GROUP_CH, axis=0)
    b_rows = lambda b: jnp.transpose(b, (0, 2, 1)).reshape(rows, p)
    out = jax.ShapeDtypeStruct((rows, p), F32)
    return pl.pallas_call(
        _s5_discretize_body,
        out_shape=(out, out, out, out),
        name="s5_discretize",
    )(rep(lam_re), rep(lam_im), rep(log_step.reshape(g, 1)), b_rows(b_re), b_rows(b_im))


def _s5_scan_body(u_ref, bblk_ref, cre_ref, cim_ref, ar_ref, ai_ref, d_ref, y_ref,
                  lhs_ref, bu_ref, x_ref, yl_ref, *, steps, n_state):
    ls = S5_CHUNK_STEPS
    n_chunks = steps // ls
    ar = jnp.broadcast_to(ar_ref[...], (S5_SEGMENTS, n_state))
    ai = jnp.broadcast_to(ai_ref[...], (S5_SEGMENTS, n_state))

    def project(c):
        def gather(i, carry):
            lhs_ref[pl.ds(i * S5_SEGMENTS, S5_SEGMENTS), :] = (
                u_ref[pl.ds(c * ls + i, S5_SEGMENTS, stride=steps), :])
            return carry
        lax.fori_loop(0, ls, gather, 0, unroll=8)
        bu_ref[...] = jnp.dot(lhs_ref[...].astype(BF16), bblk_ref[...], preferred_element_type=F32)

    def recur(state, store):
        def step(i, carry):
            xr, xi = carry
            rows = pl.ds(i * S5_SEGMENTS, S5_SEGMENTS)
            nxr = ar * xr - ai * xi + bu_ref[rows, 0:n_state]
            nxi = ar * xi + ai * xr + bu_ref[rows, n_state:2 * n_state]
            if store:
                x_ref[rows, 0:n_state] = nxr
                x_ref[rows, n_state:2 * n_state] = nxi
            return nxr, nxi
        return lax.fori_loop(0, ls, step, state, unroll=2)

    zero = jnp.zeros((S5_SEGMENTS, n_state), F32)

    def pass1(c, state):
        project(c)
        return recur(state, False)
    er, ei = lax.fori_loop(0, n_chunks, pass1, (zero, zero))

    pr, pi = ar_ref[...], ai_ref[...]
    for _ in range(int(math.log2(steps))):
        pr, pi = pr * pr - pi * pi, 2.0 * pr * pi
    cr = jnp.zeros((1, n_state), F32)
    ci = jnp.zeros((1, n_state), F32)
    init_r, init_i = [], []
    for j in range(S5_SEGMENTS):
        init_r.append(cr)
        init_i.append(ci)
        cr, ci = (er[j:j + 1] + pr * cr - pi * ci, ei[j:j + 1] + pr * ci + pi * cr)
    x0 = (jnp.concatenate(init_r, axis=0), jnp.concatenate(init_i, axis=0))

    d = d_ref[...]

    def pass2(c, state):
        project(c)
        state = recur(state, True)
        xr = x_ref[:, 0:n_state].astype(BF16)
        xi = x_ref[:, n_state:2 * n_state].astype(BF16)
        y = (jnp.dot(xr, cre_ref[...], preferred_element_type=F32)
             - jnp.dot(xi, cim_ref[...], preferred_element_type=F32))
        yl_ref[...] = jax.nn.gelu(y + d * lhs_ref[...])

        def emit(i, carry):
            y_ref[pl.ds(c * ls + i, S5_SEGMENTS, stride=steps), :] = (
                yl_ref[pl.ds(i * S5_SEGMENTS, S5_SEGMENTS), :])
            return carry
        lax.fori_loop(0, ls, emit, 0, unroll=8)
        return state
    lax.fori_loop(0, n_chunks, pass2, x0)


def s5_scan(u, ab_re, ab_im, bb_re, bb_im, c_re, c_im, d_skip):
    t, width = u.shape
    g, p = ab_re.shape
    gb = S5_GROUPS_PER_BLOCK
    n_blocks = g // gb
    n_state = gb * p
    steps = t // S5_SEGMENTS
    assert steps % S5_CHUNK_STEPS == 0 and steps & (steps - 1) == 0
    eye = jnp.eye(gb, dtype=F32)

    def blockdiag_in(bb):
        z = bb.reshape(n_blocks, gb, B_GROUP_CH, p)
        return jnp.einsum('bgcp,gh->bgchp', z, eye).reshape(n_blocks, gb * B_GROUP_CH, n_state)

    def blockdiag_out(cc):
        z = cc.reshape(n_blocks, gb, B_GROUP_CH, p)
        return jnp.einsum('bgcp,gh->bhpgc', z, eye).reshape(n_blocks, n_state, gb * B_GROUP_CH)

    bblk = jnp.concatenate([blockdiag_in(bb_re), blockdiag_in(bb_im)], axis=-1).astype(BF16)
    cre = blockdiag_out(c_re).astype(BF16)
    cim = blockdiag_out(c_im).astype(BF16)
    ar = ab_re.reshape(n_blocks, 1, n_state)
    ai = ab_im.reshape(n_blocks, 1, n_state)
    d = d_skip.reshape(n_blocks, 1, LANES)
    rows_chunk = S5_CHUNK_STEPS * S5_SEGMENTS
    return pl.pallas_call(
        functools.partial(_s5_scan_body, steps=steps, n_state=n_state),
        grid=(n_blocks,),
        in_specs=[pl.BlockSpec((t, LANES), lambda b: (0, b)),
                  pl.BlockSpec((None, LANES, 2 * n_state), lambda b: (b, 0, 0)),
                  pl.BlockSpec((None, n_state, LANES), lambda b: (b, 0, 0)),
                  pl.BlockSpec((None, n_state, LANES), lambda b: (b, 0, 0)),
                  pl.BlockSpec((None, 1, n_state), lambda b: (b, 0, 0)),
                  pl.BlockSpec((None, 1, n_state), lambda b: (b, 0, 0)),
                  pl.BlockSpec((None, 1, LANES), lambda b: (b, 0, 0))],
        out_specs=pl.BlockSpec((t, LANES), lambda b: (0, b)),
        out_shape=jax.ShapeDtypeStruct((t, width), F32),
        scratch_shapes=[pltpu.VMEM((rows_chunk, LANES), F32),
                        pltpu.VMEM((rows_chunk, 2 * n_state), F32),
                        pltpu.VMEM((rows_chunk, 2 * n_state), F32),
                        pltpu.VMEM((rows_chunk, LANES), F32)],
        compiler_params=_cparams(("parallel",)),
        name="s5_scan",
    )(u, bblk, cre, cim, ar, ai, d)


def _attn_body(q_ref, kp_ref, kc_ref, vp_ref, vc_ref, o_ref, lse_ref, *, blocks_per_residue, n_heads):
    b = pl.program_id(0)
    not_first = (b % blocks_per_residue) != 0
    qi = lax.broadcasted_iota(jnp.int32, (C_QBLOCK, 2 * C_QBLOCK), 0)
    kj = lax.broadcasted_iota(jnp.int32, (C_QBLOCK, 2 * C_QBLOCK), 1)
    dist = C_QBLOCK + qi - kj
    valid = (dist >= 0) & (dist <= C_QBLOCK) & (not_first | (kj >= C_QBLOCK))
    scale = C_HEAD_DIM ** -0.5
    for h in range(n_heads):
        cols = slice(h * C_HEAD_DIM, (h + 1) * C_HEAD_DIM)
        q = q_ref[:, cols]
        k = jnp.concatenate([kp_ref[:, cols], kc_ref[:, cols]], axis=0)
        v = jnp.concatenate([vp_ref[:, cols], vc_ref[:, cols]], axis=0)
        s = lax.dot_general(q, k, (((1,), (1,)), ((), ())), preferred_element_type=F32) * scale
        s = jnp.where(valid, s, -jnp.inf)
        m = jnp.max(s, axis=-1, keepdims=True)
        e = jnp.exp(s - m)
        ssum = jnp.sum(e, axis=-1, keepdims=True)
        p = e / ssum
        o_ref[:, cols] = jnp.dot(p.astype(BF16), v, preferred_element_type=F32).astype(o_ref.dtype)
        lse_ref[:, h:h + 1] = m + jnp.log(ssum)


def window_attention(qkv, dil):
    t, w3 = qkv.shape
    width = w3 // 3
    n_heads = width // C_HEAD_DIM
    n_blocks = t // C_QBLOCK
    cur = lambda part: pl.BlockSpec((C_QBLOCK, width), lambda b, p=part: (b, p))
    prev = lambda part: pl.BlockSpec((C_QBLOCK, width), lambda b, p=part: (jnp.maximum(b - 1, 0), p))
    return pl.pallas_call(
        functools.partial(_attn_body, blocks_per_residue=n_blocks // dil, n_heads=n_heads),
        grid=(n_blocks,),
        in_specs=[cur(0), prev(1), cur(1), prev(2), cur(2)],
        out_specs=(pl.BlockSpec((C_QBLOCK, width), lambda b: (b, 0)),
                   pl.BlockSpec((C_QBLOCK, n_heads), lambda b: (b, 0))),
        out_shape=(jax.ShapeDtypeStruct((t, width), BF16),
                   jax.ShapeDtypeStruct((t, n_heads), F32)),
        compiler_params=_cparams(("parallel",)),
        name=f"window_attention_d{dil}",
    )(qkv, qkv, qkv, qkv, qkv)


def _residue_permutation(n, dil, inverse=False):
    row = lax.broadcasted_iota(jnp.int32, (n, n), 0)
    col = lax.broadcasted_iota(jnp.int32, (n, n), 1)
    per = n // dil
    if inverse:
        hit = col == (row % dil) * per + row // dil
    else:
        hit = col == (row % per) * dil + row // per
    return jnp.where(hit, 1.0, 0.0).astype(BF16)


def _residue_copies_body(x_ref, *o_refs, dils):
    x = x_ref[...]
    tm = x.shape[0]
    for dil, o_ref in zip(dils, o_refs):
        per = tm // dil
        shuffled = jnp.dot(_residue_permutation(tm, dil), x, preferred_element_type=F32).astype(BF16)
        for r in range(dil):
            o_ref[r] = shuffled[r * per:(r + 1) * per]


def residue_major_copies(x, dils, tm=256):
    t, d = x.shape
    outs = pl.pallas_call(
        functools.partial(_residue_copies_body, dils=dils),
        grid=(t // tm,),
        in_specs=[pl.BlockSpec((tm, d), lambda i: (i, 0))],
        out_specs=tuple(pl.BlockSpec((dil, tm // dil, d), lambda i: (0, i, 0)) for dil in dils),
        out_shape=tuple(jax.ShapeDtypeStruct((dil, t // dil, d), BF16) for dil in dils),
        compiler_params=_cparams(("parallel",)),
        name="residue_major_copies",
    )(x)
    return [o.reshape(t, d) for o in outs]


def _attn_combine_body(*refs, dils):
    n = len(dils)
    o_refs, l_refs, out_ref = refs[:n], refs[n:2 * n], refs[2 * n]
    scr_refs = refs[2 * n + 1:]
    tm, width = out_ref.shape
    nat = []
    scr = iter(scr_refs)
    for dil, o_ref in zip(dils, o_refs):
        if dil == 1:
            nat.append(o_ref.at[0])
            continue
        s_ref = next(scr)
        o2d = jnp.concatenate([o_ref[r] for r in range(dil)], axis=0)
        s_ref[...] = jnp.dot(_residue_permutation(tm, dil, inverse=True), o2d,
                             preferred_element_type=F32)
        nat.append(s_ref)
    ls = [l[...] for l in l_refs]
    mx = functools.reduce(jnp.maximum, ls)
    es = [jnp.exp(l - mx) for l in ls]
    den = functools.reduce(lambda a, b: a + b, es)
    alphas = [e / den for e in es]
    n_heads = ls[0].shape[1]
    for h in range(n_heads):
        cols = slice(h * C_HEAD_DIM, (h + 1) * C_HEAD_DIM)
        acc = None
        for alpha, o in zip(alphas, nat):
            term = alpha[:, h:h + 1] * o[:, cols].astype(F32)
            acc = term if acc is None else acc + term
        out_ref[:, cols] = acc.astype(out_ref.dtype)


def attn_combine(outs, lses, dils, tm=256):
    t, width = outs[0].shape
    n_heads = lses[0].shape[1]
    o_specs = [pl.BlockSpec((dil, tm // dil, width), lambda i: (0, i, 0)) for dil in dils]
    l_spec = pl.BlockSpec((tm, n_heads), lambda i: (i, 0))
    return pl.pallas_call(
        functools.partial(_attn_combine_body, dils=dils),
        grid=(t // tm,),
        in_specs=o_specs + [l_spec] * len(dils),
        out_specs=pl.BlockSpec((tm, width), lambda i: (i, 0)),
        out_shape=jax.ShapeDtypeStruct((t, width), BF16),
        scratch_shapes=[pltpu.VMEM((tm, width), F32) for dil in dils if dil > 1],
        compiler_params=_cparams(("parallel",)),
        name="attn_combine",
    )(*[o.reshape(dil, t // dil, width) for o, dil in zip(outs, dils)], *lses)


def _from_residue_major(a, dil):
    t, d = a.shape
    return a.reshape(dil, t // dil, d).transpose(1, 0, 2).reshape(t, d)


def _to_residue_major(a, dil):
    t, d = a.shape
    return a.reshape(t // dil, dil, d).transpose(1, 0, 2).reshape(t, d)


def ffn_layer(stream, gain, w_gate, w_up, w_down, layer, emit_stats):
    h, hb, ssq = stream
    ff = w_gate.shape[2]
    mid, w_down_bf16 = matmul_ws(hb, [w_gate, w_up], layer, [0, 0], ff, _epi_swiglu, BF16,
                                 norm=(ssq, gain), side=(w_down, layer), tm=1024, tn=512,
                                 name="ffn_gate_up")
    out = matmul_rowres(mid, w_down_bf16, h, emit_stats=emit_stats, tm=1024, tn=512, tk=ff // 2,
                        name="ffn_down")
    return out if emit_stats else (out, None, None)


def gmlp_layer(stream, gain, w_in, v_gain, w_spatial, b_spatial, w_out, layer):
    h, hb, ssq = stream
    d = h.shape[1]
    uv = matmul_ws(hb, [w_in], layer, [0], w_in.shape[2], _epi_gelu, BF16, norm=(ssq, gain),
                   tm=1024, tn=1024, name="gmlp_in")
    gated = gmlp_gate(uv, v_gain, w_spatial, b_spatial)
    return matmul_ws(gated, [w_out], layer, [0], d, _epi_res, F32, res=h, emit_stats=True,
                     tm=512, tn=1024, name="gmlp_out")


def s5_layer(stream, gain, w_in, lam_re, lam_im, log_step, b_re, b_im, c_re, c_im, d_skip, w_out,
             layer):
    h, hb, ssq = stream
    d = h.shape[1]
    u = matmul_ws(hb, [w_in], layer, [0], w_in.shape[2], _epi_id, F32, norm=(ssq, gain), tm=1024,
                  tn=min(1024, w_in.shape[2]), name="s5_in")
    ab_re, ab_im, bb_re, bb_im = s5_discretize(lam_re, lam_im, log_step, b_re, b_im)
    y = s5_scan(u, ab_re[::B_GROUP_CH], ab_im[::B_GROUP_CH], bb_re, bb_im, c_re, c_im, d_skip)
    return matmul_ws(y.astype(BF16), [w_out, w_out], layer, [0, d], d, _epi_glu_res, F32, res=h,
                     emit_stats=True, tm=1024, tn=1024, name="s5_out")


def attn_layer(stream, gain, w_in, w_out, layer):
    h, hb, ssq = stream
    d = h.shape[1]
    width = w_out.shape[1]
    dils = tuple(dil for _, dil in C_PATTERNS)
    assert all(window // dil == C_QBLOCK for window, dil in C_PATTERNS) and dils[0] == 1
    hbs = [hb] + residue_major_copies(hb, dils[1:])
    tn = 1024 if (3 * width) % 1024 == 0 else 512
    outs, lses = [], []
    for p_idx, dil in enumerate(dils):
        qkv = matmul_ws(hbs[p_idx], [w_in], layer, [p_idx * 3 * width], 3 * width, _epi_id, BF16,
                        norm=(_to_residue_major(ssq, dil), gain), tm=1024, tn=tn,
                        name=f"attn_in_d{dil}")
        o, lse = window_attention(qkv, dil)
        outs.append(o)
        lses.append(_from_residue_major(lse, dil))
    mixed = attn_combine(outs, lses, dils)
    return matmul_ws(mixed, [w_out], layer, [0], d, _epi_res, F32, res=h, emit_stats=True,
                     tm=1024, tn=1024, name="attn_out")


def kernel(x, norm_mix, norm_ffn, w_gate, w_up, w_down, a_w_in, a_v_gain, a_w_spatial, a_b_spatial,
           a_w_out, b_w_in, b_lambda_re, b_lambda_im, b_log_step, b_b_re, b_b_im, b_c_re, b_c_im,
           b_d_skip, b_w_out, c_w_in, c_w_out, final_norm):
    bsz, seq, d = x.shape
    depth = norm_mix.shape[0]
    outs = []
    for bi in range(bsz):
        stream = (x[bi],) + tuple(stream_stats(x[bi]))
        for i in range(depth):
            kind, j = i % 3, i // 3
            if kind == 0:
                stream = gmlp_layer(stream, norm_mix[i], a_w_in, a_v_gain[j], a_w_spatial[j],
                                    a_b_spatial[j], a_w_out, j)
            elif kind == 1:
                stream = s5_layer(stream, norm_mix[i], b_w_in, b_lambda_re[j], b_lambda_im[j],
                                  b_log_step[j], b_b_re[j], b_b_im[j], b_c_re[j], b_c_im[j],
                                  b_d_skip[j], b_w_out, j)
            else:
                stream = attn_layer(stream, norm_mix[i], c_w_in, c_w_out, j)
            stream = ffn_layer(stream, norm_ffn[i], w_gate, w_up, w_down, i,
                               emit_stats=i + 1 < depth)
        outs.append(rmsnorm(stream[0], final_norm, x.dtype))
    return jnp.stack(outs)
```

```python
import functools
import math

import jax
import jax.numpy as jnp
from jax import lax
from jax.experimental import pallas as pl
from jax.experimental.pallas import tpu as pltpu

F32 = jnp.float32
BF16 = jnp.bfloat16

RMS_EPS = 1e-6
LN_EPS = 1e-5

LANES = 128
SUBLANES = 8
BF16_SUBLANES = 16
VMEM_CAPACITY_BYTES = 64 * 1024 * 1024
VMEM_RESERVE_BYTES = 3 * 1024 * 1024
VMEM_LIMIT_BYTES = 56 * 1024 * 1024

A_CHUNK = 128
A_GROUPS = 8
B_GROUP_CH = 16
S5_GROUPS_PER_BLOCK = LANES // B_GROUP_CH
S5_SEGMENTS = SUBLANES
S5_CHUNK_STEPS = 128
C_HEAD_DIM = 128
C_QBLOCK = 128
C_PATTERNS = ((128, 1), (512, 4), (2048, 16))


def _cparams(sem, vmem_limit_bytes=VMEM_LIMIT_BYTES):
    return pltpu.CompilerParams(dimension_semantics=sem, vmem_limit_bytes=vmem_limit_bytes)


def _rmsnorm_body(x_ref, g_ref, o_ref):
    x = x_ref[...]
    y = x * lax.rsqrt(jnp.mean(x * x, axis=-1, keepdims=True) + RMS_EPS)
    o_ref[...] = (y * g_ref[...]).astype(o_ref.dtype)


def rmsnorm(x, gain, out_dtype, tm=256):
    t, d = x.shape
    return pl.pallas_call(
        _rmsnorm_body,
        grid=(t // tm,),
        in_specs=[pl.BlockSpec((tm, d), lambda i: (i, 0)),
                  pl.BlockSpec((1, d), lambda i: (0, 0))],
        out_specs=pl.BlockSpec((tm, d), lambda i: (i, 0)),
        out_shape=jax.ShapeDtypeStruct((t, d), out_dtype),
        compiler_params=_cparams(("parallel",)),
        name="rmsnorm",
    )(x, gain.reshape(1, d))


def _lane_group_sum(x):
    out = x[:, 0:LANES]
    for g in range(1, x.shape[1] // LANES):
        out = out + x[:, g * LANES:(g + 1) * LANES]
    return out


def _rms_row_scale(ssq_ref, d_model):
    return lax.rsqrt(jnp.sum(ssq_ref[...], axis=-1, keepdims=True) / d_model + RMS_EPS)


def _stream_stats_body(x_ref, hb_ref, ssq_ref):
    x = x_ref[...]
    hb_ref[...] = x.astype(BF16)
    ssq_ref[...] = _lane_group_sum(x * x)


def stream_stats(x, tm=256):
    t, d = x.shape
    return pl.pallas_call(
        _stream_stats_body,
        grid=(t // tm,),
        in_specs=[pl.BlockSpec((tm, d), lambda i: (i, 0))],
        out_specs=(pl.BlockSpec((tm, d), lambda i: (i, 0)),
                   pl.BlockSpec((tm, LANES), lambda i: (i, 0))),
        out_shape=(jax.ShapeDtypeStruct((t, d), BF16), jax.ShapeDtypeStruct((t, LANES), F32)),
        compiler_params=_cparams(("parallel",)),
        name="stream_stats",
    )(x)


def _mm_ws_body(*refs, n_b, has_res, has_norm, emit_stats, has_side, nj, sub, last_cols, d_model,
                epilogue):
    it = iter(refs)
    a_ref = next(it)
    w_refs = [next(it) for _ in range(n_b)]
    res_ref = next(it) if has_res else None
    ssq_ref, gain_ref = (next(it), next(it)) if has_norm else (None, None)
    side_ref = next(it) if has_side else None
    o_ref = next(it)
    hb_ref, ssq_out_ref = (next(it), next(it)) if emit_stats else (None, None)
    side_out_ref = next(it) if has_side else None
    wbf_refs = [(next(it), next(it)) for _ in range(n_b)]
    ssq_acc_ref = next(it) if emit_stats else None
    jj = pl.program_id(0)
    i = pl.program_id(1)
    kc = w_refs[0].shape[0]

    def cast_into(buf):
        rows = pl.ds(pl.multiple_of(i * kc, kc), kc)
        for w_ref, wbf in zip(w_refs, wbf_refs):
            w = w_ref[...]
            if has_norm:
                w = w * gain_ref[...]
            wbf[buf][rows, :] = w.astype(BF16)

    def side_cast():
        if has_side:
            n_tiles, _, tile_cols = side_out_ref.shape
            for t in range(n_tiles):
                side_out_ref[t] = side_ref[:, t * tile_cols:(t + 1) * tile_cols].astype(BF16)

    def compute_from(buf, n_cols):
        a = a_ref[...]
        rs = _rms_row_scale(ssq_ref, d_model) if has_norm else None
        total = None
        for c in range(pl.cdiv(n_cols, sub)):
            cols = slice(c * sub, (c + 1) * sub)
            accs = [jnp.dot(a, wbf[buf][:, cols], preferred_element_type=F32) for wbf in wbf_refs]
            if has_norm:
                accs = [acc * rs for acc in accs]
            out = epilogue(accs, res_ref[:, cols] if has_res else None)
            o_ref[:, cols] = out.astype(o_ref.dtype)
            if emit_stats:
                hb_ref[:, cols] = out.astype(BF16)
                part = _lane_group_sum(out * out)
                total = part if total is None else total + part
        return total

    def add_stats(total, first):
        if emit_stats:
            ssq_acc_ref[i] = total if first else ssq_acc_ref[i] + total
            ssq_out_ref[...] = ssq_acc_ref[i]

    @pl.when(jj == 0)
    def _():
        cast_into(0)
        side_cast()

    tn = o_ref.shape[1]
    if nj > 1:
        @pl.when(jj == 1)
        def _():
            total = compute_from(0, tn)
            cast_into(1)
            side_cast()
            add_stats(total, True)

    for parity in (0, 1):
        @pl.when((jj > 1) & (jj < nj) & (jj % 2 == parity))
        def _():
            total = compute_from(1 - parity, tn)
            cast_into(parity)
            side_cast()
            add_stats(total, False)

    @pl.when(jj == nj)
    def _():
        total = compute_from((nj - 1) % 2, last_cols)
        side_cast()
        add_stats(total, nj == 1)


def matmul_ws(a, ws, layer, col_offsets, n_out, epilogue, out_dtype, *, res=None, norm=None,
              emit_stats=False, side=None, tm, tn, sub=256, name):
    m, kdim = a.shape
    ni = m // tm
    nj = pl.cdiv(n_out, tn)
    kc = kdim // ni
    assert m % tm == 0 and kdim % ni == 0 and kc % SUBLANES == 0
    assert all(off % tn == 0 for off in col_offsets)
    assert not emit_stats or (res is not None and out_dtype == F32 and n_out % tn == 0)
    row = lambda jj, i: jnp.where(jj == 0, 0, i)
    in_specs = [pl.BlockSpec((tm, kdim), lambda jj, i: (row(jj, i), 0))]
    for off in col_offsets:
        in_specs.append(pl.BlockSpec(
            (None, kc, tn), lambda jj, i, o=off // tn: (layer, i, jnp.minimum(jj, nj - 1) + o)))
    args = [a] + list(ws)
    out_spec = pl.BlockSpec((tm, tn), lambda jj, i: (row(jj, i), jnp.maximum(jj - 1, 0)))
    stat_spec = pl.BlockSpec((tm, LANES), lambda jj, i: (row(jj, i), 0))
    if res is not None:
        in_specs.append(out_spec)
        args.append(res)
    if norm is not None:
        ssq, gain = norm
        in_specs += [stat_spec, pl.BlockSpec((kc, 1), lambda jj, i: (i, 0))]
        args += [ssq, gain.reshape(kdim, 1)]
    out_specs = [out_spec]
    out_shape = [jax.ShapeDtypeStruct((m, n_out), out_dtype)]
    scratch = [pltpu.VMEM((kdim, tn), BF16) for _ in ws for _ in range(2)]
    if emit_stats:
        out_specs += [out_spec, stat_spec]
        out_shape += [jax.ShapeDtypeStruct((m, n_out), BF16), jax.ShapeDtypeStruct((m, LANES), F32)]
        scratch.append(pltpu.VMEM((ni, tm, LANES), F32))
    if side is not None:
        stack, side_layer, tile_cols = side
        _, s_rows, s_cols = stack.shape
        assert s_cols % tile_cols == 0
        sr = next(r for r in range(BF16_SUBLANES, s_rows + 1, BF16_SUBLANES)
                  if s_rows % r == 0 and s_rows // r <= (nj + 1) * ni)
        n_side = s_rows // sr
        blk = lambda jj, i: jnp.minimum(jj * ni + i, n_side - 1)
        in_specs.append(pl.BlockSpec((None, sr, s_cols), lambda jj, i: (side_layer, blk(jj, i), 0)))
        args.append(stack)
        n_tiles = s_cols // tile_cols
        out_specs.append(pl.BlockSpec((n_tiles, sr, tile_cols), lambda jj, i: (0, blk(jj, i), 0)))
        out_shape.append(jax.ShapeDtypeStruct((n_tiles, s_rows, tile_cols), BF16))
    stat_tile = tm * LANES * 4
    vmem_bytes = (2 * tm * kdim * 2 + len(ws) * (2 * kdim * tn * 2 + 2 * kc * tn * 4)
                  + 2 * tm * tn * jnp.dtype(out_dtype).itemsize + 2 * len(ws) * tm * sub * 4)
    if res is not None:
        vmem_bytes += 2 * tm * tn * 4
    if norm is not None:
        vmem_bytes += 2 * stat_tile + 2 * kc * LANES * 4
    if emit_stats:
        vmem_bytes += 2 * tm * tn * 2 + (2 + ni) * stat_tile
    if side is not None:
        vmem_bytes += 2 * sr * s_cols * (4 + 2)
    assert vmem_bytes + VMEM_RESERVE_BYTES <= VMEM_CAPACITY_BYTES
    outs = pl.pallas_call(
        functools.partial(_mm_ws_body, n_b=len(ws), has_res=res is not None, has_norm=norm is not None,
                          emit_stats=emit_stats, has_side=side is not None, nj=nj, sub=min(sub, tn),
                          last_cols=n_out - (nj - 1) * tn, d_model=kdim, epilogue=epilogue),
        grid=(nj + 1, ni),
        in_specs=in_specs,
        out_specs=tuple(out_specs),
        out_shape=tuple(out_shape),
        scratch_shapes=scratch,
        compiler_params=_cparams(("arbitrary", "arbitrary"),
                                 max(VMEM_LIMIT_BYTES, vmem_bytes + VMEM_RESERVE_BYTES)),
        name=name,
    )(*args)
    return outs[0] if len(outs) == 1 else outs


def _mm_rowres_body(*refs, nk, sub, emit_stats):
    if emit_stats:
        a_ref, b_ref, res_ref, o_ref, hb_ref, ssq_out_ref, acc_ref, ssq_acc_ref = refs
    else:
        a_ref, b_ref, res_ref, o_ref, acc_ref = refs
    k = pl.program_id(1)
    j = pl.program_id(2)

    @pl.when(k == 0)
    def _():
        acc_ref[j] = jnp.dot(a_ref[...], b_ref[...], preferred_element_type=F32)

    @pl.when((k > 0) & (k < nk - 1))
    def _():
        acc_ref[j] += jnp.dot(a_ref[...], b_ref[...], preferred_element_type=F32)

    def finish(first):
        a = a_ref[...]
        total = None
        for c in range(o_ref.shape[1] // sub):
            cols = slice(c * sub, (c + 1) * sub)
            out = (res_ref[:, cols] + acc_ref[j, :, cols]
                   + jnp.dot(a, b_ref[:, cols], preferred_element_type=F32))
            o_ref[:, cols] = out
            if emit_stats:
                hb_ref[:, cols] = out.astype(BF16)
                part = _lane_group_sum(out * out)
                total = part if total is None else total + part
        if emit_stats:
            ssq_acc_ref[...] = total if first else ssq_acc_ref[...] + total
            ssq_out_ref[...] = ssq_acc_ref[...]

    @pl.when((k == nk - 1) & (j == 0))
    def _():
        finish(True)

    @pl.when((k == nk - 1) & (j > 0))
    def _():
        finish(False)


def matmul_rowres(a, b, res, *, emit_stats, tm, tk, name):
    m, kdim = a.shape
    n_tiles, _, tn = b.shape
    n = n_tiles * tn
    nk = kdim // tk
    assert m % tm == 0 and n % tn == 0 and kdim % tk == 0 and nk >= 2
    last = lambda i, k, j: (i, jnp.where(k == nk - 1, j, 0))
    tile_f32 = tm * tn * 4
    vmem_bytes = (n // tn + 4) * tile_f32 + 2 * (tm * tk + tk * tn) * 2
    out_specs = [pl.BlockSpec((tm, tn), last)]
    out_shape = [jax.ShapeDtypeStruct((m, n), F32)]
    scratch = [pltpu.VMEM((n // tn, tm, tn), F32)]
    if emit_stats:
        out_specs += [pl.BlockSpec((tm, tn), last), pl.BlockSpec((tm, LANES), lambda i, k, j: (i, 0))]
        out_shape += [jax.ShapeDtypeStruct((m, n), BF16), jax.ShapeDtypeStruct((m, LANES), F32)]
        scratch.append(pltpu.VMEM((tm, LANES), F32))
        vmem_bytes += 2 * tm * tn * 2 + 3 * tm * LANES * 4
    assert vmem_bytes + VMEM_RESERVE_BYTES <= VMEM_CAPACITY_BYTES
    outs = pl.pallas_call(
        functools.partial(_mm_rowres_body, nk=nk, sub=min(256, tn), emit_stats=emit_stats),
        grid=(m // tm, nk, n // tn),
        in_specs=[pl.BlockSpec((tm, tk), lambda i, k, j: (i, k)),
                  pl.BlockSpec((None, tk, tn), lambda i, k, j: (j, k, 0)),
                  pl.BlockSpec((tm, tn), last, pipeline_mode=pl.Buffered(1))],
        out_specs=tuple(out_specs),
        out_shape=tuple(out_shape),
        scratch_shapes=scratch,
        compiler_params=_cparams(("parallel", "arbitrary", "arbitrary"),
                                 max(VMEM_LIMIT_BYTES, vmem_bytes + VMEM_RESERVE_BYTES)),
        name=name,
    )(a, b, res)
    return outs if emit_stats else outs[0]


def _epi_id(accs, res):
    return accs[0]


def _epi_gelu(accs, res):
    return jax.nn.gelu(accs[0])


def _epi_res(accs, res):
    return res + accs[0]


def _epi_swiglu(accs, res):
    return jax.nn.silu(accs[0]) * accs[1]


def _epi_glu_res(accs, res):
    return res + accs[0] * jax.nn.sigmoid(accs[1])


def _gmlp_gate_body(u_ref, v_ref, gain_ref, ws_ref, bs_ref, o_ref, *, n_chunks, group_dim):
    v = v_ref[...].astype(F32)
    v = v - jnp.mean(v, axis=-1, keepdims=True)
    v = v * lax.rsqrt(jnp.mean(v * v, axis=-1, keepdims=True) + LN_EPS)
    vn = (v * gain_ref[...]).astype(BF16)
    row = lax.broadcasted_iota(jnp.int32, (A_CHUNK, A_CHUNK), 0)
    col = lax.broadcasted_iota(jnp.int32, (A_CHUNK, A_CHUNK), 1)
    causal = col <= row
    n_groups = ws_ref.shape[0]
    for g in range(n_groups):
        w = jnp.where(causal, ws_ref[g], 0.0).astype(BF16)
        bias = bs_ref[:, g:g + 1]
        cols = slice(g * group_dim, (g + 1) * group_dim)
        for c in range(n_chunks):
            rows = slice(c * A_CHUNK, (c + 1) * A_CHUNK)
            mixed = jnp.dot(w, vn[rows, cols], preferred_element_type=F32) + bias
            o_ref[rows, cols] = (u_ref[rows, cols].astype(F32) * mixed).astype(o_ref.dtype)


def gmlp_gate(uv, v_gain, w_spatial, b_spatial, n_chunks=2):
    t, w2 = uv.shape
    width = w2 // 2
    tm = n_chunks * A_CHUNK
    n_groups = w_spatial.shape[0]
    return pl.pallas_call(
        functools.partial(_gmlp_gate_body, n_chunks=n_chunks, group_dim=width // n_groups),
        grid=(t // tm,),
        in_specs=[pl.BlockSpec((tm, width), lambda i: (i, 0)),
                  pl.BlockSpec((tm, width), lambda i: (i, 1)),
                  pl.BlockSpec((1, width), lambda i: (0, 0)),
                  pl.BlockSpec((n_groups, A_CHUNK, A_CHUNK), lambda i: (0, 0, 0)),
                  pl.BlockSpec((A_CHUNK, n_groups), lambda i: (0, 0))],
        out_specs=pl.BlockSpec((tm, width), lambda i: (i, 0)),
        out_shape=jax.ShapeDtypeStruct((t, width), BF16),
        compiler_params=_cparams(("parallel",)),
        name="gmlp_gate",
    )(uv, uv, v_gain.reshape(1, width), w_spatial, b_spatial.T)


def _s5_discretize_body(lr_ref, li_ref, ls_ref, br_ref, bi_ref, abr_ref, abi_ref, bbr_ref, bbi_ref):
    lr, li = lr_ref[...], li_ref[...]
    dt = jnp.exp(ls_ref[...])
    decay = jnp.exp(lr * dt)
    ab_re, ab_im = decay * jnp.cos(li * dt), decay * jnp.sin(li * dt)
    den = lr * lr + li * li
    nr, ni = ab_re - 1.0, ab_im
    coef_re = (nr * lr + ni * li) / den
    coef_im = (ni * lr - nr * li) / den
    br, bi = br_ref[...], bi_ref[...]
    abr_ref[...] = ab_re
    abi_ref[...] = ab_im
    bbr_ref[...] = coef_re * br - coef_im * bi
    bbi_ref[...] = coef_re * bi + coef_im * br


def s5_discretize(lam_re, lam_im, log_step, b_re, b_im):
    g, p = lam_re.shape
    rows = g * B_GROUP_CH
    rep = lambda z: jnp.repeat(z, B_GROUP_CH, axis=0)
    b_rows = lambda b: jnp.transpose(b, (0, 2, 1)).reshape(rows, p)
    out = jax.ShapeDtypeStruct((rows, p), F32)
    return pl.pallas_call(
        _s5_discretize_body,
        out_shape=(out, out, out, out),
        name="s5_discretize",
    )(rep(lam_re), rep(lam_im), rep(log_step.reshape(g, 1)), b_rows(b_re), b_rows(b_im))


def _s5_scan_body(u_ref, bblk_ref, cre_ref, cim_ref, ar_ref, ai_ref, d_ref, y_ref,
                  lhs_ref, bu_ref, x_ref, yl_ref, *, steps, n_state):
    ls = S5_CHUNK_STEPS
    n_chunks = steps // ls
    ar = jnp.broadcast_to(ar_ref[...], (S5_SEGMENTS, n_state))
    ai = jnp.broadcast_to(ai_ref[...], (S5_SEGMENTS, n_state))

    def project(c):
        def gather(i, carry):
            lhs_ref[pl.ds(i * S5_SEGMENTS, S5_SEGMENTS), :] = (
                u_ref[pl.ds(c * ls + i, S5_SEGMENTS, stride=steps), :])
            return carry
        lax.fori_loop(0, ls, gather, 0, unroll=8)
        bu_ref[...] = jnp.dot(lhs_ref[...].astype(BF16), bblk_ref[...], preferred_element_type=F32)

    def recur(state, store):
        def step(i, carry):
            xr, xi = carry
            rows = pl.ds(i * S5_SEGMENTS, S5_SEGMENTS)
            nxr = ar * xr - ai * xi + bu_ref[rows, 0:n_state]
            nxi = ar * xi + ai * xr + bu_ref[rows, n_state:2 * n_state]
            if store:
                x_ref[rows, 0:n_state] = nxr
                x_ref[rows, n_state:2 * n_state] = nxi
            return nxr, nxi
        return lax.fori_loop(0, ls, step, state, unroll=2)

    zero = jnp.zeros((S5_SEGMENTS, n_state), F32)

    def pass1(c, state):
        project(c)
        return recur(state, False)
    er, ei = lax.fori_loop(0, n_chunks, pass1, (zero, zero))

    pr, pi = ar_ref[...], ai_ref[...]
    for _ in range(int(math.log2(steps))):
        pr, pi = pr * pr - pi * pi, 2.0 * pr * pi
    cr = jnp.zeros((1, n_state), F32)
    ci = jnp.zeros((1, n_state), F32)
    init_r, init_i = [], []
    for j in range(S5_SEGMENTS):
        init_r.append(cr)
        init_i.append(ci)
        cr, ci = (er[j:j + 1] + pr * cr - pi * ci, ei[j:j + 1] + pr * ci + pi * cr)
    x0 = (jnp.concatenate(init_r, axis=0), jnp.concatenate(init_i, axis=0))

    d = d_ref[...]

    def pass2(c, state):
        project(c)
        state = recur(state, True)
        xr = x_ref[:, 0:n_state].astype(BF16)
        xi = x_ref[:, n_state:2 * n_state].astype(BF16)
        y = (jnp.dot(xr, cre_ref[...], preferred_element_type=F32)
             - jnp.dot(xi, cim_ref[...], preferred_element_type=F32))
        yl_ref[...] = jax.nn.gelu(y + d * lhs_ref[...])

        def emit(i, carry):
            y_ref[pl.ds(c * ls + i, S5_SEGMENTS, stride=steps), :] = (
                yl_ref[pl.ds(i * S5_SEGMENTS, S5_SEGMENTS), :])
            return carry
        lax.fori_loop(0, ls, emit, 0, unroll=8)
        return state
    lax.fori_loop(0, n_chunks, pass2, x0)


def s5_scan(u, ab_re, ab_im, bb_re, bb_im, c_re, c_im, d_skip):
    t, width = u.shape
    g, p = ab_re.shape
    gb = S5_GROUPS_PER_BLOCK
    n_blocks = g // gb
    n_state = gb * p
    steps = t // S5_SEGMENTS
    assert steps % S5_CHUNK_STEPS == 0 and steps & (steps - 1) == 0
    eye = jnp.eye(gb, dtype=F32)

    def blockdiag_in(bb):
        z = bb.reshape(n_blocks, gb, B_GROUP_CH, p)
        return jnp.einsum('bgcp,gh->bgchp', z, eye).reshape(n_blocks, gb * B_GROUP_CH, n_state)

    def blockdiag_out(cc):
        z = cc.reshape(n_blocks, gb, B_GROUP_CH, p)
        return jnp.einsum('bgcp,gh->bhpgc', z, eye).reshape(n_blocks, n_state, gb * B_GROUP_CH)

    bblk = jnp.concatenate([blockdiag_in(bb_re), blockdiag_in(bb_im)], axis=-1).astype(BF16)
    cre = blockdiag_out(c_re).astype(BF16)
    cim = blockdiag_out(c_im).astype(BF16)
    ar = ab_re.reshape(n_blocks, 1, n_state)
    ai = ab_im.reshape(n_blocks, 1, n_state)
    d = d_skip.reshape(n_blocks, 1, LANES)
    rows_chunk = S5_CHUNK_STEPS * S5_SEGMENTS
    return pl.pallas_call(
        functools.partial(_s5_scan_body, steps=steps, n_state=n_state),
        grid=(n_blocks,),
        in_specs=[pl.BlockSpec((t, LANES), lambda b: (0, b)),
                  pl.BlockSpec((None, LANES, 2 * n_state), lambda b: (b, 0, 0)),
                  pl.BlockSpec((None, n_state, LANES), lambda b: (b, 0, 0)),
                  pl.BlockSpec((None, n_state, LANES), lambda b: (b, 0, 0)),
                  pl.BlockSpec((None, 1, n_state), lambda b: (b, 0, 0)),
                  pl.BlockSpec((None, 1, n_state), lambda b: (b, 0, 0)),
                  pl.BlockSpec((None, 1, LANES), lambda b: (b, 0, 0))],
        out_specs=pl.BlockSpec((t, LANES), lambda b: (0, b)),
        out_shape=jax.ShapeDtypeStruct((t, width), F32),
        scratch_shapes=[pltpu.VMEM((rows_chunk, LANES), F32),
                        pltpu.VMEM((rows_chunk, 2 * n_state), F32),
                        pltpu.VMEM((rows_chunk, 2 * n_state), F32),
                        pltpu.VMEM((rows_chunk, LANES), F32)],
        compiler_params=_cparams(("parallel",)),
        name="s5_scan",
    )(u, bblk, cre, cim, ar, ai, d)


def _attn_body(q_ref, kp_ref, kc_ref, vp_ref, vc_ref, o_ref, lse_ref, *, blocks_per_residue, n_heads):
    b = pl.program_id(0)
    not_first = (b % blocks_per_residue) != 0
    qi = lax.broadcasted_iota(jnp.int32, (C_QBLOCK, 2 * C_QBLOCK), 0)
    kj = lax.broadcasted_iota(jnp.int32, (C_QBLOCK, 2 * C_QBLOCK), 1)
    dist = C_QBLOCK + qi - kj
    valid = (dist >= 0) & (dist <= C_QBLOCK) & (not_first | (kj >= C_QBLOCK))
    scale = C_HEAD_DIM ** -0.5
    for h in range(n_heads):
        cols = slice(h * C_HEAD_DIM, (h + 1) * C_HEAD_DIM)
        q = q_ref[:, cols]
        k = jnp.concatenate([kp_ref[:, cols], kc_ref[:, cols]], axis=0)
        v = jnp.concatenate([vp_ref[:, cols], vc_ref[:, cols]], axis=0)
        s = lax.dot_general(q, k, (((1,), (1,)), ((), ())), preferred_element_type=F32) * scale
        s = jnp.where(valid, s, -jnp.inf)
        m = jnp.max(s, axis=-1, keepdims=True)
        e = jnp.exp(s - m)
        ssum = jnp.sum(e, axis=-1, keepdims=True)
        p = e / ssum
        o_ref[:, cols] = jnp.dot(p.astype(BF16), v, preferred_element_type=F32).astype(o_ref.dtype)
        lse_ref[:, h:h + 1] = m + jnp.log(ssum)


def window_attention(qkv, dil):
    t, w3 = qkv.shape
    width = w3 // 3
    n_heads = width // C_HEAD_DIM
    n_blocks = t // C_QBLOCK
    cur = lambda part: pl.BlockSpec((C_QBLOCK, width), lambda b, p=part: (b, p))
    prev = lambda part: pl.BlockSpec((C_QBLOCK, width), lambda b, p=part: (jnp.maximum(b - 1, 0), p))
    return pl.pallas_call(
        functools.partial(_attn_body, blocks_per_residue=n_blocks // dil, n_heads=n_heads),
        grid=(n_blocks,),
        in_specs=[cur(0), prev(1), cur(1), prev(2), cur(2)],
        out_specs=(pl.BlockSpec((C_QBLOCK, width), lambda b: (b, 0)),
                   pl.BlockSpec((C_QBLOCK, n_heads), lambda b: (b, 0))),
        out_shape=(jax.ShapeDtypeStruct((t, width), BF16),
                   jax.ShapeDtypeStruct((t, n_heads), F32)),
        compiler_params=_cparams(("parallel",)),
        name=f"window_attention_d{dil}",
    )(qkv, qkv, qkv, qkv, qkv)


def _residue_permutation(n, dil, inverse=False):
    row = lax.broadcasted_iota(jnp.int32, (n, n), 0)
    col = lax.broadcasted_iota(jnp.int32, (n, n), 1)
    per = n // dil
    if inverse:
        hit = col == (row % dil) * per + row // dil
    else:
        hit = col == (row % per) * dil + row // per
    return jnp.where(hit, 1.0, 0.0).astype(BF16)


def _residue_copies_body(x_ref, *o_refs, dils):
    x = x_ref[...]
    tm = x.shape[0]
    for dil, o_ref in zip(dils, o_refs):
        per = tm // dil
        shuffled = jnp.dot(_residue_permutation(tm, dil), x, preferred_element_type=F32).astype(BF16)
        for r in range(dil):
            o_ref[r] = shuffled[r * per:(r + 1) * per]


def residue_major_copies(x, dils, tm=256):
    t, d = x.shape
    outs = pl.pallas_call(
        functools.partial(_residue_copies_body, dils=dils),
        grid=(t // tm,),
        in_specs=[pl.BlockSpec((tm, d), lambda i: (i, 0))],
        out_specs=tuple(pl.BlockSpec((dil, tm // dil, d), lambda i: (0, i, 0)) for dil in dils),
        out_shape=tuple(jax.ShapeDtypeStruct((dil, t // dil, d), BF16) for dil in dils),
        compiler_params=_cparams(("parallel",)),
        name="residue_major_copies",
    )(x)
    return [o.reshape(t, d) for o in outs]


def _attn_combine_body(*refs, dils):
    n = len(dils)
    o_refs, l_refs, out_ref = refs[:n], refs[n:2 * n], refs[2 * n]
    scr_refs = refs[2 * n + 1:]
    tm, width = out_ref.shape
    nat = []
    scr = iter(scr_refs)
    for dil, o_ref in zip(dils, o_refs):
        if dil == 1:
            nat.append(o_ref.at[0])
            continue
        s_ref = next(scr)
        o2d = jnp.concatenate([o_ref[r] for r in range(dil)], axis=0)
        s_ref[...] = jnp.dot(_residue_permutation(tm, dil, inverse=True), o2d,
                             preferred_element_type=F32)
        nat.append(s_ref)
    ls = [l[...] for l in l_refs]
    mx = functools.reduce(jnp.maximum, ls)
    es = [jnp.exp(l - mx) for l in ls]
    den = functools.reduce(lambda a, b: a + b, es)
    alphas = [e / den for e in es]
    n_heads = ls[0].shape[1]
    for h in range(n_heads):
        cols = slice(h * C_HEAD_DIM, (h + 1) * C_HEAD_DIM)
        acc = None
        for alpha, o in zip(alphas, nat):
            term = alpha[:, h:h + 1] * o[:, cols].astype(F32)
            acc = term if acc is None else acc + term
        out_ref[:, cols] = acc.astype(out_ref.dtype)


def attn_combine(outs, lses, dils, tm=256):
    t, width = outs[0].shape
    n_heads = lses[0].shape[1]
    o_specs = [pl.BlockSpec((dil, tm // dil, width), lambda i: (0, i, 0)) for dil in dils]
    l_spec = pl.BlockSpec((tm, n_heads), lambda i: (i, 0))
    return pl.pallas_call(
        functools.partial(_attn_combine_body, dils=dils),
        grid=(t // tm,),
        in_specs=o_specs + [l_spec] * len(dils),
        out_specs=pl.BlockSpec((tm, width), lambda i: (i, 0)),
        out_shape=jax.ShapeDtypeStruct((t, width), BF16),
        scratch_shapes=[pltpu.VMEM((tm, width), F32) for dil in dils if dil > 1],
        compiler_params=_cparams(("parallel",)),
        name="attn_combine",
    )(*[o.reshape(dil, t // dil, width) for o, dil in zip(outs, dils)], *lses)


def _from_residue_major(a, dil):
    t, d = a.shape
    return a.reshape(dil, t // dil, d).transpose(1, 0, 2).reshape(t, d)


def _to_residue_major(a, dil):
    t, d = a.shape
    return a.reshape(t // dil, dil, d).transpose(1, 0, 2).reshape(t, d)


def ffn_layer(stream, gain, w_gate, w_up, w_down, layer, emit_stats):
    h, hb, ssq = stream
    ff = w_gate.shape[2]
    mid, w_down_bf16 = matmul_ws(hb, [w_gate, w_up], layer, [0, 0], ff, _epi_swiglu, BF16,
                                 norm=(ssq, gain), side=(w_down, layer, 512), tm=1024, tn=512,
                                 name="ffn_gate_up")
    out = matmul_rowres(mid, w_down_bf16, h, emit_stats=emit_stats, tm=1024, tk=ff // 2,
                        name="ffn_down")
    return out if emit_stats else (out, None, None)


def gmlp_layer(stream, gain, w_in, v_gain, w_spatial, b_spatial, w_out, layer):
    h, hb, ssq = stream
    d = h.shape[1]
    uv = matmul_ws(hb, [w_in], layer, [0], w_in.shape[2], _epi_gelu, BF16, norm=(ssq, gain),
                   tm=1024, tn=1024, name="gmlp_in")
    gated = gmlp_gate(uv, v_gain, w_spatial, b_spatial)
    return matmul_ws(gated, [w_out], layer, [0], d, _epi_res, F32, res=h, emit_stats=True,
                     tm=512, tn=1024, name="gmlp_out")


def s5_layer(stream, gain, w_in, lam_re, lam_im, log_step, b_re, b_im, c_re, c_im, d_skip, w_out,
             layer):
    h, hb, ssq = stream
    d = h.shape[1]
    u = matmul_ws(hb, [w_in], layer, [0], w_in.shape[2], _epi_id, F32, norm=(ssq, gain), tm=1024,
                  tn=min(1024, w_in.shape[2]), name="s5_in")
    ab_re, ab_im, bb_re, bb_im = s5_discretize(lam_re, lam_im, log_step, b_re, b_im)
    y = s5_scan(u, ab_re[::B_GROUP_CH], ab_im[::B_GROUP_CH], bb_re, bb_im, c_re, c_im, d_skip)
    return matmul_ws(y.astype(BF16), [w_out, w_out], layer, [0, d], d, _epi_glu_res, F32, res=h,
                     emit_stats=True, tm=1024, tn=1024, name="s5_out")


def attn_layer(stream, gain, w_in, w_out, layer):
    h, hb, ssq = stream
    d = h.shape[1]
    width = w_out.shape[1]
    dils = tuple(dil for _, dil in C_PATTERNS)
    assert all(window // dil == C_QBLOCK for window, dil in C_PATTERNS) and dils[0] == 1
    hbs = [hb] + residue_major_copies(hb, dils[1:])
    tn = 1024 if (3 * width) % 1024 == 0 else 512
    outs, lses = [], []
    for p_idx, dil in enumerate(dils):
        qkv = matmul_ws(hbs[p_idx], [w_in], layer, [p_idx * 3 * width], 3 * width, _epi_id, BF16,
                        norm=(_to_residue_major(ssq, dil), gain), tm=1024, tn=tn,
                        name=f"attn_in_d{dil}")
        o, lse = window_attention(qkv, dil)
        outs.append(o)
        lses.append(_from_residue_major(lse, dil))
    mixed = attn_combine(outs, lses, dils)
    return matmul_ws(mixed, [w_out], layer, [0], d, _epi_res, F32, res=h, emit_stats=True,
                     tm=1024, tn=1024, name="attn_out")


def kernel(x, norm_mix, norm_ffn, w_gate, w_up, w_down, a_w_in, a_v_gain, a_w_spatial, a_b_spatial,
           a_w_out, b_w_in, b_lambda_re, b_lambda_im, b_log_step, b_b_re, b_b_im, b_c_re, b_c_im,
           b_d_skip, b_w_out, c_w_in, c_w_out, final_norm):
    bsz, seq, d = x.shape
    depth = norm_mix.shape[0]
    outs = []
    for bi in range(bsz):
        stream = (x[bi],) + tuple(stream_stats(x[bi]))
        for i in range(depth):
            kind, j = i % 3, i // 3
            if kind == 0:
                stream = gmlp_layer(stream, norm_mix[i], a_w_in, a_v_gain[j], a_w_spatial[j],
                                    a_b_spatial[j], a_w_out, j)
            elif kind == 1:
                stream = s5_layer(stream, norm_mix[i], b_w_in, b_lambda_re[j], b_lambda_im[j],
                                  b_log_step[j], b_b_re[j], b_b_im[j], b_c_re[j], b_c_im[j],
                                  b_d_skip[j], b_w_out, j)
            else:
                stream = attn_layer(stream, norm_mix[i], c_w_in, c_w_out, j)
            stream = ffn_layer(stream, norm_ffn[i], w_gate, w_up, w_down, i,
                               emit_stats=i + 1 < depth)
        outs.append(rmsnorm(stream[0], final_norm, x.dtype))
    return jnp.stack(outs)
```

```python
import functools
import math

import jax
import jax.numpy as jnp
from jax import lax
from jax.experimental import pallas as pl
from jax.experimental.pallas import tpu as pltpu

F32 = jnp.float32
BF16 = jnp.bfloat16

RMS_EPS = 1e-6
LN_EPS = 1e-5

LANES = 128
SUBLANES = 8
BF16_SUBLANES = 16
VMEM_CAPACITY_BYTES = 64 * 1024 * 1024
VMEM_RESERVE_BYTES = 3 * 1024 * 1024
VMEM_LIMIT_BYTES = 56 * 1024 * 1024

A_CHUNK = 128
A_GROUPS = 8
B_GROUP_CH = 16
S5_GROUPS_PER_BLOCK = LANES // B_GROUP_CH
S5_SEGMENTS = SUBLANES
S5_CHUNK_STEPS = 128
C_HEAD_DIM = 128
C_QBLOCK = 128
C_PATTERNS = ((128, 1), (512, 4), (2048, 16))


def _cparams(sem, vmem_limit_bytes=VMEM_LIMIT_BYTES):
    return pltpu.CompilerParams(dimension_semantics=sem, vmem_limit_bytes=vmem_limit_bytes)


def _rmsnorm_body(x_ref, g_ref, o_ref):
    x = x_ref[...]
    y = x * lax.rsqrt(jnp.mean(x * x, axis=-1, keepdims=True) + RMS_EPS)
    o_ref[...] = (y * g_ref[...]).astype(o_ref.dtype)


def rmsnorm(x, gain, out_dtype, tm=256):
    t, d = x.shape
    return pl.pallas_call(
        _rmsnorm_body,
        grid=(t // tm,),
        in_specs=[pl.BlockSpec((tm, d), lambda i: (i, 0)),
                  pl.BlockSpec((1, d), lambda i: (0, 0))],
        out_specs=pl.BlockSpec((tm, d), lambda i: (i, 0)),
        out_shape=jax.ShapeDtypeStruct((t, d), out_dtype),
        compiler_params=_cparams(("parallel",)),
        name="rmsnorm",
    )(x, gain.reshape(1, d))


def _lane_group_sum(x):
    out = x[:, 0:LANES]
    for g in range(1, x.shape[1] // LANES):
        out = out + x[:, g * LANES:(g + 1) * LANES]
    return out


def _rms_row_scale(ssq_ref, d_model):
    return lax.rsqrt(jnp.sum(ssq_ref[...], axis=-1, keepdims=True) / d_model + RMS_EPS)


def _stream_stats_body(x_ref, hb_ref, ssq_ref):
    x = x_ref[...]
    hb_ref[...] = x.astype(BF16)
    ssq_ref[...] = _lane_group_sum(x * x)


def stream_stats(x, tm=256):
    t, d = x.shape
    return pl.pallas_call(
        _stream_stats_body,
        grid=(t // tm,),
        in_specs=[pl.BlockSpec((tm, d), lambda i: (i, 0))],
        out_specs=(pl.BlockSpec((tm, d), lambda i: (i, 0)),
                   pl.BlockSpec((tm, LANES), lambda i: (i, 0))),
        out_shape=(jax.ShapeDtypeStruct((t, d), BF16), jax.ShapeDtypeStruct((t, LANES), F32)),
        compiler_params=_cparams(("parallel",)),
        name="stream_stats",
    )(x)


def _mm_ws_body(*refs, n_b, has_res, has_norm, emit_stats, has_side, nj, sub, last_cols, d_model,
                epilogue):
    it = iter(refs)
    a_ref = next(it)
    w_refs = [next(it) for _ in range(n_b)]
    res_ref = next(it) if has_res else None
    ssq_ref, gain_ref = (next(it), next(it)) if has_norm else (None, None)
    side_ref = next(it) if has_side else None
    o_ref = next(it)
    hb_ref, ssq_out_ref = (next(it), next(it)) if emit_stats else (None, None)
    side_out_ref = next(it) if has_side else None
    wbf_refs = [(next(it), next(it)) for _ in range(n_b)]
    ssq_acc_ref = next(it) if emit_stats else None
    jj = pl.program_id(0)
    i = pl.program_id(1)
    kc = w_refs[0].shape[0]

    def cast_into(buf):
        rows = pl.ds(pl.multiple_of(i * kc, kc), kc)
        for w_ref, wbf in zip(w_refs, wbf_refs):
            w = w_ref[...]
            if has_norm:
                w = w * gain_ref[...]
            wbf[buf][rows, :] = w.astype(BF16)

    def side_cast():
        if has_side:
            n_tiles, _, tile_cols = side_out_ref.shape
            for t in range(n_tiles):
                side_out_ref[t] = side_ref[:, t * tile_cols:(t + 1) * tile_cols].astype(BF16)

    def compute_from(buf, n_cols):
        a = a_ref[...]
        rs = _rms_row_scale(ssq_ref, d_model) if has_norm else None
        total = None
        for c in range(pl.cdiv(n_cols, sub)):
            cols = slice(c * sub, (c + 1) * sub)
            accs = [jnp.dot(a, wbf[buf][:, cols], preferred_element_type=F32) for wbf in wbf_refs]
            if has_norm:
                accs = [acc * rs for acc in accs]
            out = epilogue(accs, res_ref[:, cols] if has_res else None)
            o_ref[:, cols] = out.astype(o_ref.dtype)
            if emit_stats:
                hb_ref[:, cols] = out.astype(BF16)
                part = _lane_group_sum(out * out)
                total = part if total is None else total + part
        return total

    def add_stats(total, first, last=False):
        if emit_stats:
            total = total if first else ssq_acc_ref[i] + total
            if last:
                ssq_out_ref[...] = total
            else:
                ssq_acc_ref[i] = total

    @pl.when(jj == 0)
    def _():
        cast_into(0)
        side_cast()

    tn = o_ref.shape[1]
    if nj > 1:
        @pl.when(jj == 1)
        def _():
            total = compute_from(0, tn)
            cast_into(1)
            side_cast()
            add_stats(total, True)

    for parity in (0, 1):
        @pl.when((jj > 1) & (jj < nj) & (jj % 2 == parity))
        def _():
            total = compute_from(1 - parity, tn)
            cast_into(parity)
            side_cast()
            add_stats(total, False)

    @pl.when(jj == nj)
    def _():
        total = compute_from((nj - 1) % 2, last_cols)
        side_cast()
        add_stats(total, nj == 1, last=True)


def matmul_ws(a, ws, layer, col_offsets, n_out, epilogue, out_dtype, *, res=None, norm=None,
              emit_stats=False, side=None, tm, tn, sub=256, name):
    m, kdim = a.shape
    ni = m // tm
    nj = pl.cdiv(n_out, tn)
    kc = kdim // ni
    assert m % tm == 0 and kdim % ni == 0 and kc % SUBLANES == 0
    assert all(off % tn == 0 for off in col_offsets)
    assert not emit_stats or (res is not None and out_dtype == F32 and n_out % tn == 0)
    row = lambda jj, i: jnp.where(jj == 0, 0, i)
    in_specs = [pl.BlockSpec((tm, kdim), lambda jj, i: (row(jj, i), 0))]
    for off in col_offsets:
        in_specs.append(pl.BlockSpec(
            (None, kc, tn), lambda jj, i, o=off // tn: (layer, i, jnp.minimum(jj, nj - 1) + o)))
    args = [a] + list(ws)
    out_spec = pl.BlockSpec((tm, tn), lambda jj, i: (row(jj, i), jnp.maximum(jj - 1, 0)))
    stat_spec = pl.BlockSpec((tm, LANES), lambda jj, i: (row(jj, i), 0))
    if res is not None:
        in_specs.append(out_spec)
        args.append(res)
    if norm is not None:
        ssq, gain = norm
        in_specs += [stat_spec, pl.BlockSpec((kc, 1), lambda jj, i: (i, 0))]
        args += [ssq, gain.reshape(kdim, 1)]
    out_specs = [out_spec]
    out_shape = [jax.ShapeDtypeStruct((m, n_out), out_dtype)]
    scratch = [pltpu.VMEM((kdim, tn), BF16) for _ in ws for _ in range(2)]
    if emit_stats:
        out_specs += [out_spec, pl.BlockSpec((tm, LANES), lambda jj, i: (jnp.where(jj == nj, i, 0), 0))]
        out_shape += [jax.ShapeDtypeStruct((m, n_out), BF16), jax.ShapeDtypeStruct((m, LANES), F32)]
        scratch.append(pltpu.VMEM((ni, tm, LANES), F32))
    if side is not None:
        stack, side_layer, tile_cols = side
        _, s_rows, s_cols = stack.shape
        assert s_cols % tile_cols == 0
        sr = next(r for r in range(BF16_SUBLANES, s_rows + 1, BF16_SUBLANES)
                  if s_rows % r == 0 and s_rows // r <= (nj + 1) * ni)
        n_side = s_rows // sr
        blk = lambda jj, i: jnp.minimum(jj * ni + i, n_side - 1)
        in_specs.append(pl.BlockSpec((None, sr, s_cols), lambda jj, i: (side_layer, blk(jj, i), 0)))
        args.append(stack)
        n_tiles = s_cols // tile_cols
        out_specs.append(pl.BlockSpec((n_tiles, sr, tile_cols), lambda jj, i: (0, blk(jj, i), 0)))
        out_shape.append(jax.ShapeDtypeStruct((n_tiles, s_rows, tile_cols), BF16))
    stat_tile = tm * LANES * 4
    vmem_bytes = (2 * tm * kdim * 2 + len(ws) * (2 * kdim * tn * 2 + 2 * kc * tn * 4)
                  + 2 * tm * tn * jnp.dtype(out_dtype).itemsize + 2 * len(ws) * tm * sub * 4)
    if res is not None:
        vmem_bytes += 2 * tm * tn * 4
    if norm is not None:
        vmem_bytes += 2 * stat_tile + 2 * kc * LANES * 4
    if emit_stats:
        vmem_bytes += 2 * tm * tn * 2 + (2 + ni) * stat_tile
    if side is not None:
        vmem_bytes += 2 * sr * s_cols * (4 + 2)
    assert vmem_bytes + VMEM_RESERVE_BYTES <= VMEM_CAPACITY_BYTES
    outs = pl.pallas_call(
        functools.partial(_mm_ws_body, n_b=len(ws), has_res=res is not None, has_norm=norm is not None,
                          emit_stats=emit_stats, has_side=side is not None, nj=nj, sub=min(sub, tn),
                          last_cols=n_out - (nj - 1) * tn, d_model=kdim, epilogue=epilogue),
        grid=(nj + 1, ni),
        in_specs=in_specs,
        out_specs=tuple(out_specs),
        out_shape=tuple(out_shape),
        scratch_shapes=scratch,
        compiler_params=_cparams(("arbitrary", "arbitrary"),
                                 max(VMEM_LIMIT_BYTES, vmem_bytes + VMEM_RESERVE_BYTES)),
        name=name,
    )(*args)
    return outs[0] if len(outs) == 1 else outs


def _mm_rowres_body(*refs, nk, sub, emit_stats):
    if emit_stats:
        a_ref, b_ref, res_ref, o_ref, hb_ref, ssq_out_ref, acc_ref, ssq_acc_ref = refs
    else:
        a_ref, b_ref, res_ref, o_ref, acc_ref = refs
    k = pl.program_id(1)
    j = pl.program_id(2)

    @pl.when(k == 0)
    def _():
        acc_ref[j] = jnp.dot(a_ref[...], b_ref[...], preferred_element_type=F32)

    @pl.when((k > 0) & (k < nk - 1))
    def _():
        acc_ref[j] += jnp.dot(a_ref[...], b_ref[...], preferred_element_type=F32)

    def finish(first):
        a = a_ref[...]
        total = None
        for c in range(o_ref.shape[1] // sub):
            cols = slice(c * sub, (c + 1) * sub)
            out = (res_ref[:, cols] + acc_ref[j, :, cols]
                   + jnp.dot(a, b_ref[:, cols], preferred_element_type=F32))
            o_ref[:, cols] = out
            if emit_stats:
                hb_ref[:, cols] = out.astype(BF16)
                part = _lane_group_sum(out * out)
                total = part if total is None else total + part
        if emit_stats:
            ssq_acc_ref[...] = total if first else ssq_acc_ref[...] + total
            ssq_out_ref[...] = ssq_acc_ref[...]

    @pl.when((k == nk - 1) & (j == 0))
    def _():
        finish(True)

    @pl.when((k == nk - 1) & (j > 0))
    def _():
        finish(False)


def matmul_rowres(a, b, res, *, emit_stats, tm, tk, name):
    m, kdim = a.shape
    n_tiles, _, tn = b.shape
    n = n_tiles * tn
    nk = kdim // tk
    assert m % tm == 0 and n % tn == 0 and kdim % tk == 0 and nk >= 2
    last = lambda i, k, j: (i, jnp.where(k == nk - 1, j, 0))
    tile_f32 = tm * tn * 4
    vmem_bytes = (n // tn + 5) * tile_f32 + (tm * tk + 2 * tk * tn) * 2
    out_specs = [pl.BlockSpec((tm, tn), last)]
    out_shape = [jax.ShapeDtypeStruct((m, n), F32)]
    scratch = [pltpu.VMEM((n // tn, tm, tn), F32)]
    if emit_stats:
        out_specs += [pl.BlockSpec((tm, tn), last), pl.BlockSpec((tm, LANES), lambda i, k, j: (i, 0))]
        out_shape += [jax.ShapeDtypeStruct((m, n), BF16), jax.ShapeDtypeStruct((m, LANES), F32)]
        scratch.append(pltpu.VMEM((tm, LANES), F32))
        vmem_bytes += 2 * tm * tn * 2 + 3 * tm * LANES * 4
    assert vmem_bytes + VMEM_RESERVE_BYTES <= VMEM_CAPACITY_BYTES
    outs = pl.pallas_call(
        functools.partial(_mm_rowres_body, nk=nk, sub=min(256, tn), emit_stats=emit_stats),
        grid=(m // tm, nk, n // tn),
        in_specs=[pl.BlockSpec((tm, tk), lambda i, k, j: (i, k), pipeline_mode=pl.Buffered(1)),
                  pl.BlockSpec((None, tk, tn), lambda i, k, j: (j, k, 0)),
                  pl.BlockSpec((tm, tn), last)],
        out_specs=tuple(out_specs),
        out_shape=tuple(out_shape),
        scratch_shapes=scratch,
        compiler_params=_cparams(("parallel", "arbitrary", "arbitrary"),
                                 max(VMEM_LIMIT_BYTES, vmem_bytes + VMEM_RESERVE_BYTES)),
        name=name,
    )(a, b, res)
    return outs if emit_stats else outs[0]


def _epi_id(accs, res):
    return accs[0]


def _epi_gelu(accs, res):
    return jax.nn.gelu(accs[0])


def _epi_res(accs, res):
    return res + accs[0]


def _epi_swiglu(accs, res):
    return jax.nn.silu(accs[0]) * accs[1]


def _epi_glu_res(accs, res):
    return res + accs[0] * jax.nn.sigmoid(accs[1])


def _gmlp_gate_body(u_ref, v_ref, gain_ref, ws_ref, bs_ref, o_ref, *, n_chunks, group_dim):
    v = v_ref[...].astype(F32)
    v = v - jnp.mean(v, axis=-1, keepdims=True)
    v = v * lax.rsqrt(jnp.mean(v * v, axis=-1, keepdims=True) + LN_EPS)
    vn = (v * gain_ref[...]).astype(BF16)
    row = lax.broadcasted_iota(jnp.int32, (A_CHUNK, A_CHUNK), 0)
    col = lax.broadcasted_iota(jnp.int32, (A_CHUNK, A_CHUNK), 1)
    causal = col <= row
    n_groups = ws_ref.shape[0]
    for g in range(n_groups):
        w = jnp.where(causal, ws_ref[g], 0.0).astype(BF16)
        bias = bs_ref[:, g:g + 1]
        cols = slice(g * group_dim, (g + 1) * group_dim)
        for c in range(n_chunks):
            rows = slice(c * A_CHUNK, (c + 1) * A_CHUNK)
            mixed = jnp.dot(w, vn[rows, cols], preferred_element_type=F32) + bias
            o_ref[rows, cols] = (u_ref[rows, cols].astype(F32) * mixed).astype(o_ref.dtype)


def gmlp_gate(uv, v_gain, w_spatial, b_spatial, n_chunks=2):
    t, w2 = uv.shape
    width = w2 // 2
    tm = n_chunks * A_CHUNK
    n_groups = w_spatial.shape[0]
    return pl.pallas_call(
        functools.partial(_gmlp_gate_body, n_chunks=n_chunks, group_dim=width // n_groups),
        grid=(t // tm,),
        in_specs=[pl.BlockSpec((tm, width), lambda i: (i, 0)),
                  pl.BlockSpec((tm, width), lambda i: (i, 1)),
                  pl.BlockSpec((1, width), lambda i: (0, 0)),
                  pl.BlockSpec((n_groups, A_CHUNK, A_CHUNK), lambda i: (0, 0, 0)),
                  pl.BlockSpec((A_CHUNK, n_groups), lambda i: (0, 0))],
        out_specs=pl.BlockSpec((tm, width), lambda i: (i, 0)),
        out_shape=jax.ShapeDtypeStruct((t, width), BF16),
        compiler_params=_cparams(("parallel",)),
        name="gmlp_gate",
    )(uv, uv, v_gain.reshape(1, width), w_spatial, b_spatial.T)


def _s5_discretize_body(lr_ref, li_ref, ls_ref, br_ref, bi_ref, abr_ref, abi_ref, bbr_ref, bbi_ref):
    lr, li = lr_ref[...], li_ref[...]
    dt = jnp.exp(ls_ref[...])
    decay = jnp.exp(lr * dt)
    ab_re, ab_im = decay * jnp.cos(li * dt), decay * jnp.sin(li * dt)
    den = lr * lr + li * li
    nr, ni = ab_re - 1.0, ab_im
    coef_re = (nr * lr + ni * li) / den
    coef_im = (ni * lr - nr * li) / den
    br, bi = br_ref[...], bi_ref[...]
    abr_ref[...] = ab_re
    abi_ref[...] = ab_im
    bbr_ref[...] = coef_re * br - coef_im * bi
    bbi_ref[...] = coef_re * bi + coef_im * br


def s5_discretize(lam_re, lam_im, log_step, b_re, b_im):
    g, p = lam_re.shape
    rows = g * B_GROUP_CH
    rep = lambda z: jnp.repeat(z, B_GROUP_CH, axis=0)
    b_rows = lambda b: jnp.transpose(b, (0, 2, 1)).reshape(rows, p)
    out = jax.ShapeDtypeStruct((rows, p), F32)
    return pl.pallas_call(
        _s5_discretize_body,
        out_shape=(out, out, out, out),
        name="s5_discretize",
    )(rep(lam_re), rep(lam_im), rep(log_step.reshape(g, 1)), b_rows(b_re), b_rows(b_im))


def _s5_scan_body(u_ref, bblk_ref, cre_ref, cim_ref, ar_ref, ai_ref, d_ref, y_ref,
                  lhs_ref, bu_ref, x_ref, yl_ref, *, steps, n_state):
    ls = S5_CHUNK_STEPS
    n_chunks = steps // ls
    ar = jnp.broadcast_to(ar_ref[...], (S5_SEGMENTS, n_state))
    ai = jnp.broadcast_to(ai_ref[...], (S5_SEGMENTS, n_state))

    rows_chunk = ls * S5_SEGMENTS

    def chunk_rows(c):
        return pl.ds(pl.multiple_of(c * rows_chunk, rows_chunk), rows_chunk)

    def project(c, gather_rows):
        if gather_rows:
            def gather(i, carry):
                lhs_ref[pl.ds(c * rows_chunk + i * S5_SEGMENTS, S5_SEGMENTS), :] = (
                    u_ref[pl.ds(c * ls + i, S5_SEGMENTS, stride=steps), :])
                return carry
            lax.fori_loop(0, ls, gather, 0, unroll=8)
        bu_ref[...] = jnp.dot(lhs_ref[chunk_rows(c), :].astype(BF16), bblk_ref[...],
                              preferred_element_type=F32)

    def recur(state, store):
        def step(i, carry):
            xr, xi = carry
            rows = pl.ds(i * S5_SEGMENTS, S5_SEGMENTS)
            nxr = ar * xr - ai * xi + bu_ref[rows, 0:n_state]
            nxi = ar * xi + ai * xr + bu_ref[rows, n_state:2 * n_state]
            if store:
                x_ref[rows, 0:n_state] = nxr
                x_ref[rows, n_state:2 * n_state] = nxi
            return nxr, nxi
        return lax.fori_loop(0, ls, step, state, unroll=2)

    zero = jnp.zeros((S5_SEGMENTS, n_state), F32)

    def pass1(c, state):
        project(c, True)
        return recur(state, False)
    er, ei = lax.fori_loop(0, n_chunks, pass1, (zero, zero))

    pr, pi = ar_ref[...], ai_ref[...]
    for _ in range(int(math.log2(steps))):
        pr, pi = pr * pr - pi * pi, 2.0 * pr * pi
    cr = jnp.zeros((1, n_state), F32)
    ci = jnp.zeros((1, n_state), F32)
    init_r, init_i = [], []
    for j in range(S5_SEGMENTS):
        init_r.append(cr)
        init_i.append(ci)
        cr, ci = (er[j:j + 1] + pr * cr - pi * ci, ei[j:j + 1] + pr * ci + pi * cr)
    x0 = (jnp.concatenate(init_r, axis=0), jnp.concatenate(init_i, axis=0))

    d = d_ref[...]

    def pass2(c, state):
        project(c, False)
        state = recur(state, True)
        xr = x_ref[:, 0:n_state].astype(BF16)
        xi = x_ref[:, n_state:2 * n_state].astype(BF16)
        y = (jnp.dot(xr, cre_ref[...], preferred_element_type=F32)
             - jnp.dot(xi, cim_ref[...], preferred_element_type=F32))
        yl_ref[...] = jax.nn.gelu(y + d * lhs_ref[chunk_rows(c), :])

        def emit(i, carry):
            y_ref[pl.ds(c * ls + i, S5_SEGMENTS, stride=steps), :] = (
                yl_ref[pl.ds(i * S5_SEGMENTS, S5_SEGMENTS), :])
            return carry
        lax.fori_loop(0, ls, emit, 0, unroll=8)
        return state
    lax.fori_loop(0, n_chunks, pass2, x0)


def s5_scan(u, ab_re, ab_im, bb_re, bb_im, c_re, c_im, d_skip):
    t, width = u.shape
    g, p = ab_re.shape
    gb = S5_GROUPS_PER_BLOCK
    n_blocks = g // gb
    n_state = gb * p
    steps = t // S5_SEGMENTS
    assert steps % S5_CHUNK_STEPS == 0 and steps & (steps - 1) == 0
    eye = jnp.eye(gb, dtype=F32)

    def blockdiag_in(bb):
        z = bb.reshape(n_blocks, gb, B_GROUP_CH, p)
        return jnp.einsum('bgcp,gh->bgchp', z, eye).reshape(n_blocks, gb * B_GROUP_CH, n_state)

    def blockdiag_out(cc):
        z = cc.reshape(n_blocks, gb, B_GROUP_CH, p)
        return jnp.einsum('bgcp,gh->bhpgc', z, eye).reshape(n_blocks, n_state, gb * B_GROUP_CH)

    bblk = jnp.concatenate([blockdiag_in(bb_re), blockdiag_in(bb_im)], axis=-1).astype(BF16)
    cre = blockdiag_out(c_re).astype(BF16)
    cim = blockdiag_out(c_im).astype(BF16)
    ar = ab_re.reshape(n_blocks, 1, n_state)
    ai = ab_im.reshape(n_blocks, 1, n_state)
    d = d_skip.reshape(n_blocks, 1, LANES)
    rows_chunk = S5_CHUNK_STEPS * S5_SEGMENTS
    return pl.pallas_call(
        functools.partial(_s5_scan_body, steps=steps, n_state=n_state),
        grid=(n_blocks,),
        in_specs=[pl.BlockSpec((t, LANES), lambda b: (0, b)),
                  pl.BlockSpec((None, LANES, 2 * n_state), lambda b: (b, 0, 0)),
                  pl.BlockSpec((None, n_state, LANES), lambda b: (b, 0, 0)),
                  pl.BlockSpec((None, n_state, LANES), lambda b: (b, 0, 0)),
                  pl.BlockSpec((None, 1, n_state), lambda b: (b, 0, 0)),
                  pl.BlockSpec((None, 1, n_state), lambda b: (b, 0, 0)),
                  pl.BlockSpec((None, 1, LANES), lambda b: (b, 0, 0))],
        out_specs=pl.BlockSpec((t, LANES), lambda b: (0, b)),
        out_shape=jax.ShapeDtypeStruct((t, width), F32),
        scratch_shapes=[pltpu.VMEM((t, LANES), F32),
                        pltpu.VMEM((rows_chunk, 2 * n_state), F32),
                        pltpu.VMEM((rows_chunk, 2 * n_state), F32),
                        pltpu.VMEM((rows_chunk, LANES), F32)],
        compiler_params=_cparams(("parallel",)),
        name="s5_scan",
    )(u, bblk, cre, cim, ar, ai, d)


def _attn_body(q_ref, kp_ref, kc_ref, vp_ref, vc_ref, o_ref, lse_ref, *, blocks_per_residue, n_heads):
    b = pl.program_id(0)
    not_first = (b % blocks_per_residue) != 0
    qi = lax.broadcasted_iota(jnp.int32, (C_QBLOCK, 2 * C_QBLOCK), 0)
    kj = lax.broadcasted_iota(jnp.int32, (C_QBLOCK, 2 * C_QBLOCK), 1)
    dist = C_QBLOCK + qi - kj
    valid = (dist >= 0) & (dist <= C_QBLOCK) & (not_first | (kj >= C_QBLOCK))
    scale = C_HEAD_DIM ** -0.5
    for h in range(n_heads):
        cols = slice(h * C_HEAD_DIM, (h + 1) * C_HEAD_DIM)
        q = q_ref[:, cols]
        k = jnp.concatenate([kp_ref[:, cols], kc_ref[:, cols]], axis=0)
        v = jnp.concatenate([vp_ref[:, cols], vc_ref[:, cols]], axis=0)
        s = lax.dot_general(q, k, (((1,), (1,)), ((), ())), preferred_element_type=F32) * scale
        s = jnp.where(valid, s, -jnp.inf)
        m = jnp.max(s, axis=-1, keepdims=True)
        e = jnp.exp(s - m)
        ssum = jnp.sum(e, axis=-1, keepdims=True)
        p = e / ssum
        o_ref[:, cols] = jnp.dot(p.astype(BF16), v, preferred_element_type=F32).astype(o_ref.dtype)
        lse_ref[:, h:h + 1] = m + jnp.log(ssum)


def window_attention(qkv, dil):
    t, w3 = qkv.shape
    width = w3 // 3
    n_heads = width // C_HEAD_DIM
    n_blocks = t // C_QBLOCK
    cur = lambda part: pl.BlockSpec((C_QBLOCK, width), lambda b, p=part: (b, p))
    prev = lambda part: pl.BlockSpec((C_QBLOCK, width), lambda b, p=part: (jnp.maximum(b - 1, 0), p))
    return pl.pallas_call(
        functools.partial(_attn_body, blocks_per_residue=n_blocks // dil, n_heads=n_heads),
        grid=(n_blocks,),
        in_specs=[cur(0), prev(1), cur(1), prev(2), cur(2)],
        out_specs=(pl.BlockSpec((C_QBLOCK, width), lambda b: (b, 0)),
                   pl.BlockSpec((C_QBLOCK, n_heads), lambda b: (b, 0))),
        out_shape=(jax.ShapeDtypeStruct((t, width), BF16),
                   jax.ShapeDtypeStruct((t, n_heads), F32)),
        compiler_params=_cparams(("parallel",)),
        name=f"window_attention_d{dil}",
    )(qkv, qkv, qkv, qkv, qkv)


def _residue_permutation(n, dil, inverse=False):
    row = lax.broadcasted_iota(jnp.int32, (n, n), 0)
    col = lax.broadcasted_iota(jnp.int32, (n, n), 1)
    per = n // dil
    if inverse:
        hit = col == (row % dil) * per + row // dil
    else:
        hit = col == (row % per) * dil + row // per
    return jnp.where(hit, 1.0, 0.0).astype(BF16)


def _residue_copies_body(x_ref, *o_refs, dils):
    x = x_ref[...]
    tm = x.shape[0]
    for dil, o_ref in zip(dils, o_refs):
        per = tm // dil
        shuffled = jnp.dot(_residue_permutation(tm, dil), x, preferred_element_type=F32).astype(BF16)
        for r in range(dil):
            o_ref[r] = shuffled[r * per:(r + 1) * per]


def residue_major_copies(x, dils, tm=256):
    t, d = x.shape
    outs = pl.pallas_call(
        functools.partial(_residue_copies_body, dils=dils),
        grid=(t // tm,),
        in_specs=[pl.BlockSpec((tm, d), lambda i: (i, 0))],
        out_specs=tuple(pl.BlockSpec((dil, tm // dil, d), lambda i: (0, i, 0)) for dil in dils),
        out_shape=tuple(jax.ShapeDtypeStruct((dil, t // dil, d), BF16) for dil in dils),
        compiler_params=_cparams(("parallel",)),
        name="residue_major_copies",
    )(x)
    return [o.reshape(t, d) for o in outs]


def _attn_combine_body(*refs, dils):
    n = len(dils)
    o_refs, l_refs, out_ref = refs[:n], refs[n:2 * n], refs[2 * n]
    scr_refs = refs[2 * n + 1:]
    tm, width = out_ref.shape
    nat = []
    scr = iter(scr_refs)
    for dil, o_ref in zip(dils, o_refs):
        if dil == 1:
            nat.append(o_ref.at[0])
            continue
        s_ref = next(scr)
        o2d = jnp.concatenate([o_ref[r] for r in range(dil)], axis=0)
        s_ref[...] = jnp.dot(_residue_permutation(tm, dil, inverse=True), o2d,
                             preferred_element_type=F32)
        nat.append(s_ref)
    ls = [l[...] for l in l_refs]
    mx = functools.reduce(jnp.maximum, ls)
    es = [jnp.exp(l - mx) for l in ls]
    den = functools.reduce(lambda a, b: a + b, es)
    alphas = [e / den for e in es]
    n_heads = ls[0].shape[1]
    for h in range(n_heads):
        cols = slice(h * C_HEAD_DIM, (h + 1) * C_HEAD_DIM)
        acc = None
        for alpha, o in zip(alphas, nat):
            term = alpha[:, h:h + 1] * o[:, cols].astype(F32)
            acc = term if acc is None else acc + term
        out_ref[:, cols] = acc.astype(out_ref.dtype)


def attn_combine(outs, lses, dils, tm=256):
    t, width = outs[0].shape
    n_heads = lses[0].shape[1]
    o_specs = [pl.BlockSpec((dil, tm // dil, width), lambda i: (0, i, 0)) for dil in dils]
    l_spec = pl.BlockSpec((tm, n_heads), lambda i: (i, 0))
    return pl.pallas_call(
        functools.partial(_attn_combine_body, dils=dils),
        grid=(t // tm,),
        in_specs=o_specs + [l_spec] * len(dils),
        out_specs=pl.BlockSpec((tm, width), lambda i: (i, 0)),
        out_shape=jax.ShapeDtypeStruct((t, width), BF16),
        scratch_shapes=[pltpu.VMEM((tm, width), F32) for dil in dils if dil > 1],
        compiler_params=_cparams(("parallel",)),
        name="attn_combine",
    )(*[o.reshape(dil, t // dil, width) for o, dil in zip(outs, dils)], *lses)


def _from_residue_major(a, dil):
    t, d = a.shape
    return a.reshape(dil, t // dil, d).transpose(1, 0, 2).reshape(t, d)


def _to_residue_major(a, dil):
    t, d = a.shape
    return a.reshape(t // dil, dil, d).transpose(1, 0, 2).reshape(t, d)


def ffn_layer(stream, gain, w_gate, w_up, w_down, layer, emit_stats):
    h, hb, ssq = stream
    ff = w_gate.shape[2]
    mid, w_down_bf16 = matmul_ws(hb, [w_gate, w_up], layer, [0, 0], ff, _epi_swiglu, BF16,
                                 norm=(ssq, gain), side=(w_down, layer, 512), tm=1024, tn=512,
                                 name="ffn_gate_up")
    out = matmul_rowres(mid, w_down_bf16, h, emit_stats=emit_stats, tm=1024, tk=ff // 2,
                        name="ffn_down")
    return out if emit_stats else (out, None, None)


def gmlp_layer(stream, gain, w_in, v_gain, w_spatial, b_spatial, w_out, layer):
    h, hb, ssq = stream
    d = h.shape[1]
    uv = matmul_ws(hb, [w_in], layer, [0], w_in.shape[2], _epi_gelu, BF16, norm=(ssq, gain),
                   tm=1024, tn=1024, name="gmlp_in")
    gated = gmlp_gate(uv, v_gain, w_spatial, b_spatial)
    return matmul_ws(gated, [w_out], layer, [0], d, _epi_res, F32, res=h, emit_stats=True,
                     tm=512, tn=1024, name="gmlp_out")


def s5_layer(stream, gain, w_in, lam_re, lam_im, log_step, b_re, b_im, c_re, c_im, d_skip, w_out,
             layer):
    h, hb, ssq = stream
    d = h.shape[1]
    u = matmul_ws(hb, [w_in], layer, [0], w_in.shape[2], _epi_id, F32, norm=(ssq, gain), tm=1024,
                  tn=min(1024, w_in.shape[2]), name="s5_in")
    ab_re, ab_im, bb_re, bb_im = s5_discretize(lam_re, lam_im, log_step, b_re, b_im)
    y = s5_scan(u, ab_re[::B_GROUP_CH], ab_im[::B_GROUP_CH], bb_re, bb_im, c_re, c_im, d_skip)
    return matmul_ws(y.astype(BF16), [w_out, w_out], layer, [0, d], d, _epi_glu_res, F32, res=h,
                     emit_stats=True, tm=1024, tn=1024, name="s5_out")


def attn_layer(stream, gain, w_in, w_out, layer):
    h, hb, ssq = stream
    d = h.shape[1]
    width = w_out.shape[1]
    dils = tuple(dil for _, dil in C_PATTERNS)
    assert all(window // dil == C_QBLOCK for window, dil in C_PATTERNS) and dils[0] == 1
    hbs = [hb] + residue_major_copies(hb, dils[1:])
    tn = 1024 if (3 * width) % 1024 == 0 else 512
    outs, lses = [], []
    for p_idx, dil in enumerate(dils):
        qkv = matmul_ws(hbs[p_idx], [w_in], layer, [p_idx * 3 * width], 3 * width, _epi_id, BF16,
                        norm=(_to_residue_major(ssq, dil), gain), tm=1024, tn=tn,
                        name=f"attn_in_d{dil}")
        o, lse = window_attention(qkv, dil)
        outs.append(o)
        lses.append(_from_residue_major(lse, dil))
    mixed = attn_combine(outs, lses, dils)
    return matmul_ws(mixed, [w_out], layer, [0], d, _epi_res, F32, res=h, emit_stats=True,
                     tm=1024, tn=1024, name="attn_out")


def kernel(x, norm_mix, norm_ffn, w_gate, w_up, w_down, a_w_in, a_v_gain, a_w_spatial, a_b_spatial,
           a_w_out, b_w_in, b_lambda_re, b_lambda_im, b_log_step, b_b_re, b_b_im, b_c_re, b_c_im,
           b_d_skip, b_w_out, c_w_in, c_w_out, final_norm):
    bsz, seq, d = x.shape
    depth = norm_mix.shape[0]
    outs = []
    for bi in range(bsz):
        stream = (x[bi],) + tuple(stream_stats(x[bi]))
        for i in range(depth):
            kind, j = i % 3, i // 3
            if kind == 0:
                stream = gmlp_layer(stream, norm_mix[i], a_w_in, a_v_gain[j], a_w_spatial[j],
                                    a_b_spatial[j], a_w_out, j)
            elif kind == 1:
                stream = s5_layer(stream, norm_mix[i], b_w_in, b_lambda_re[j], b_lambda_im[j],
                                  b_log_step[j], b_b_re[j], b_b_im[j], b_c_re[j], b_c_im[j],
                                  b_d_skip[j], b_w_out, j)
            else:
                stream = attn_layer(stream, norm_mix[i], c_w_in, c_w_out, j)
            stream = ffn_layer(stream, norm_ffn[i], w_gate, w_up, w_down, i,
                               emit_stats=i + 1 < depth)
        outs.append(rmsnorm(stream[0], final_norm, x.dtype))
    return jnp.stack(outs)
```

```python
import functools
import math

import jax
import jax.numpy as jnp
from jax import lax
from jax.experimental import pallas as pl
from jax.experimental.pallas import tpu as pltpu

F32 = jnp.float32
BF16 = jnp.bfloat16

RMS_EPS = 1e-6
LN_EPS = 1e-5

LANES = 128
SUBLANES = 8
BF16_SUBLANES = 16
VMEM_CAPACITY_BYTES = 64 * 1024 * 1024
VMEM_RESERVE_BYTES = 3 * 1024 * 1024
VMEM_LIMIT_BYTES = 56 * 1024 * 1024

A_CHUNK = 128
A_GROUPS = 8
B_GROUP_CH = 16
S5_GROUPS_PER_BLOCK = LANES // B_GROUP_CH
S5_SEGMENTS = SUBLANES
S5_CHUNK_STEPS = 128
C_HEAD_DIM = 128
C_QBLOCK = 128
C_PATTERNS = ((128, 1), (512, 4), (2048, 16))


def _cparams(sem, vmem_limit_bytes=VMEM_LIMIT_BYTES):
    return pltpu.CompilerParams(dimension_semantics=sem, vmem_limit_bytes=vmem_limit_bytes)


def _rmsnorm_body(x_ref, g_ref, o_ref):
    x = x_ref[...]
    y = x * lax.rsqrt(jnp.mean(x * x, axis=-1, keepdims=True) + RMS_EPS)
    o_ref[...] = (y * g_ref[...]).astype(o_ref.dtype)


def rmsnorm(x, gain, out_dtype, tm=256):
    t, d = x.shape
    return pl.pallas_call(
        _rmsnorm_body,
        grid=(t // tm,),
        in_specs=[pl.BlockSpec((tm, d), lambda i: (i, 0)),
                  pl.BlockSpec((1, d), lambda i: (0, 0))],
        out_specs=pl.BlockSpec((tm, d), lambda i: (i, 0)),
        out_shape=jax.ShapeDtypeStruct((t, d), out_dtype),
        compiler_params=_cparams(("parallel",)),
        name="rmsnorm",
    )(x, gain.reshape(1, d))


def _lane_group_sum(x):
    out = x[:, 0:LANES]
    for g in range(1, x.shape[1] // LANES):
        out = out + x[:, g * LANES:(g + 1) * LANES]
    return out


def _rms_row_scale(ssq_ref, d_model):
    return lax.rsqrt(jnp.sum(ssq_ref[...], axis=-1, keepdims=True) / d_model + RMS_EPS)


def _stream_stats_body(x_ref, hb_ref, ssq_ref):
    x = x_ref[...]
    hb_ref[...] = x.astype(BF16)
    ssq_ref[...] = _lane_group_sum(x * x)


def stream_stats(x, tm=256):
    t, d = x.shape
    return pl.pallas_call(
        _stream_stats_body,
        grid=(t // tm,),
        in_specs=[pl.BlockSpec((tm, d), lambda i: (i, 0))],
        out_specs=(pl.BlockSpec((tm, d), lambda i: (i, 0)),
                   pl.BlockSpec((tm, LANES), lambda i: (i, 0))),
        out_shape=(jax.ShapeDtypeStruct((t, d), BF16), jax.ShapeDtypeStruct((t, LANES), F32)),
        compiler_params=_cparams(("parallel",)),
        name="stream_stats",
    )(x)


def _mm_ws_body(*refs, n_b, has_res, has_norm, emit_stats, has_side, nj, sub, last_cols, d_model,
                epilogue):
    it = iter(refs)
    a_ref = next(it)
    w_refs = [next(it) for _ in range(n_b)]
    res_ref = next(it) if has_res else None
    ssq_ref, gain_ref = (next(it), next(it)) if has_norm else (None, None)
    side_ref = next(it) if has_side else None
    o_ref = next(it)
    hb_ref, ssq_out_ref = (next(it), next(it)) if emit_stats else (None, None)
    side_out_ref = next(it) if has_side else None
    wbf_refs = [(next(it), next(it)) for _ in range(n_b)]
    ssq_acc_ref = next(it) if emit_stats else None
    jj = pl.program_id(0)
    i = pl.program_id(1)
    kc = w_refs[0].shape[0]

    def cast_into(buf):
        rows = pl.ds(pl.multiple_of(i * kc, kc), kc)
        for w_ref, wbf in zip(w_refs, wbf_refs):
            w = w_ref[...]
            if has_norm:
                w = w * gain_ref[...]
            wbf[buf][rows, :] = w.astype(BF16)

    def side_cast():
        if has_side:
            n_tiles, _, tile_cols = side_out_ref.shape
            for t in range(n_tiles):
                side_out_ref[t] = side_ref[:, t * tile_cols:(t + 1) * tile_cols].astype(BF16)

    def compute_from(buf, n_cols):
        a = a_ref[...]
        rs = _rms_row_scale(ssq_ref, d_model) if has_norm else None
        total = None
        for c in range(pl.cdiv(n_cols, sub)):
            cols = slice(c * sub, (c + 1) * sub)
            accs = [jnp.dot(a, wbf[buf][:, cols], preferred_element_type=F32) for wbf in wbf_refs]
            if has_norm:
                accs = [acc * rs for acc in accs]
            out = epilogue(accs, res_ref[:, cols] if has_res else None)
            o_ref[:, cols] = out.astype(o_ref.dtype)
            if emit_stats:
                hb_ref[:, cols] = out.astype(BF16)
                part = _lane_group_sum(out * out)
                total = part if total is None else total + part
        return total

    def add_stats(total, first, last=False):
        if emit_stats:
            total = total if first else ssq_acc_ref[i] + total
            if last:
                ssq_out_ref[...] = total
            else:
                ssq_acc_ref[i] = total

    @pl.when(jj == 0)
    def _():
        cast_into(0)
        side_cast()

    tn = o_ref.shape[1]
    if nj > 1:
        @pl.when(jj == 1)
        def _():
            total = compute_from(0, tn)
            cast_into(1)
            side_cast()
            add_stats(total, True)

    for parity in (0, 1):
        @pl.when((jj > 1) & (jj < nj) & (jj % 2 == parity))
        def _():
            total = compute_from(1 - parity, tn)
            cast_into(parity)
            side_cast()
            add_stats(total, False)

    @pl.when(jj == nj)
    def _():
        total = compute_from((nj - 1) % 2, last_cols)
        side_cast()
        add_stats(total, nj == 1, last=True)


def matmul_ws(a, ws, layer, col_offsets, n_out, epilogue, out_dtype, *, res=None, norm=None,
              emit_stats=False, side=None, tm, tn, sub=256, name):
    m, kdim = a.shape
    ni = m // tm
    nj = pl.cdiv(n_out, tn)
    kc = kdim // ni
    assert m % tm == 0 and kdim % ni == 0 and kc % SUBLANES == 0
    assert all(off % tn == 0 for off in col_offsets)
    assert not emit_stats or (res is not None and out_dtype == F32 and n_out % tn == 0)
    row = lambda jj, i: jnp.where(jj == 0, 0, i)
    in_specs = [pl.BlockSpec((tm, kdim), lambda jj, i: (row(jj, i), 0))]
    for off in col_offsets:
        in_specs.append(pl.BlockSpec(
            (None, kc, tn), lambda jj, i, o=off // tn: (layer, i, jnp.minimum(jj, nj - 1) + o)))
    args = [a] + list(ws)
    out_spec = pl.BlockSpec((tm, tn), lambda jj, i: (row(jj, i), jnp.maximum(jj - 1, 0)))
    stat_spec = pl.BlockSpec((tm, LANES), lambda jj, i: (row(jj, i), 0))
    if res is not None:
        in_specs.append(out_spec)
        args.append(res)
    if norm is not None:
        ssq, gain = norm
        in_specs += [stat_spec, pl.BlockSpec((kc, 1), lambda jj, i: (i, 0))]
        args += [ssq, gain.reshape(kdim, 1)]
    out_specs = [out_spec]
    out_shape = [jax.ShapeDtypeStruct((m, n_out), out_dtype)]
    scratch = [pltpu.VMEM((kdim, tn), BF16) for _ in ws for _ in range(2)]
    if emit_stats:
        out_specs += [out_spec, pl.BlockSpec((tm, LANES), lambda jj, i: (jnp.where(jj == nj, i, 0), 0))]
        out_shape += [jax.ShapeDtypeStruct((m, n_out), BF16), jax.ShapeDtypeStruct((m, LANES), F32)]
        scratch.append(pltpu.VMEM((ni, tm, LANES), F32))
    if side is not None:
        stack, side_layer, tile_cols = side
        _, s_rows, s_cols = stack.shape
        assert s_cols % tile_cols == 0
        sr = next(r for r in range(BF16_SUBLANES, s_rows + 1, BF16_SUBLANES)
                  if s_rows % r == 0 and s_rows // r <= (nj + 1) * ni)
        n_side = s_rows // sr
        blk = lambda jj, i: jnp.minimum(jj * ni + i, n_side - 1)
        in_specs.append(pl.BlockSpec((None, sr, s_cols), lambda jj, i: (side_layer, blk(jj, i), 0)))
        args.append(stack)
        n_tiles = s_cols // tile_cols
        out_specs.append(pl.BlockSpec((n_tiles, sr, tile_cols), lambda jj, i: (0, blk(jj, i), 0)))
        out_shape.append(jax.ShapeDtypeStruct((n_tiles, s_rows, tile_cols), BF16))
    stat_tile = tm * LANES * 4
    vmem_bytes = (2 * tm * kdim * 2 + len(ws) * (2 * kdim * tn * 2 + 2 * kc * tn * 4)
                  + 2 * tm * tn * jnp.dtype(out_dtype).itemsize + 2 * len(ws) * tm * sub * 4)
    if res is not None:
        vmem_bytes += 2 * tm * tn * 4
    if norm is not None:
        vmem_bytes += 2 * stat_tile + 2 * kc * LANES * 4
    if emit_stats:
        vmem_bytes += 2 * tm * tn * 2 + (2 + ni) * stat_tile
    if side is not None:
        vmem_bytes += 2 * sr * s_cols * (4 + 2)
    assert vmem_bytes + VMEM_RESERVE_BYTES <= VMEM_CAPACITY_BYTES
    outs = pl.pallas_call(
        functools.partial(_mm_ws_body, n_b=len(ws), has_res=res is not None, has_norm=norm is not None,
                          emit_stats=emit_stats, has_side=side is not None, nj=nj, sub=min(sub, tn),
                          last_cols=n_out - (nj - 1) * tn, d_model=kdim, epilogue=epilogue),
        grid=(nj + 1, ni),
        in_specs=in_specs,
        out_specs=tuple(out_specs),
        out_shape=tuple(out_shape),
        scratch_shapes=scratch,
        compiler_params=_cparams(("arbitrary", "arbitrary"),
                                 max(VMEM_LIMIT_BYTES, vmem_bytes + VMEM_RESERVE_BYTES)),
        name=name,
    )(*args)
    return outs[0] if len(outs) == 1 else outs


def _mm_rowres_body(*refs, nk, n_split, emit_stats):
    it = iter(refs)
    a_ref = next(it)
    b_refs = [next(it) for _ in range(n_split)]
    res_ref = next(it)
    o_ref = next(it)
    hb_ref, ssq_out_ref = (next(it), next(it)) if emit_stats else (None, None)
    acc_ref = next(it)
    ssq_acc_ref = next(it) if emit_stats else None
    k = pl.program_id(1)
    j = pl.program_id(2)
    sub = b_refs[0].shape[1]

    def accumulate(first):
        a = a_ref[...]
        for c, b_ref in enumerate(b_refs):
            cols = slice(c * sub, (c + 1) * sub)
            part = jnp.dot(a, b_ref[...], preferred_element_type=F32)
            acc_ref[j, :, cols] = part if first else acc_ref[j, :, cols] + part

    @pl.when(k == 0)
    def _():
        accumulate(True)

    @pl.when((k > 0) & (k < nk - 1))
    def _():
        accumulate(False)

    def finish(first):
        a = a_ref[...]
        total = None
        for c, b_ref in enumerate(b_refs):
            cols = slice(c * sub, (c + 1) * sub)
            out = (res_ref[:, cols] + acc_ref[j, :, cols]
                   + jnp.dot(a, b_ref[...], preferred_element_type=F32))
            o_ref[:, cols] = out
            if emit_stats:
                hb_ref[:, cols] = out.astype(BF16)
                part = _lane_group_sum(out * out)
                total = part if total is None else total + part
        if emit_stats:
            ssq_acc_ref[...] = total if first else ssq_acc_ref[...] + total
            ssq_out_ref[...] = ssq_acc_ref[...]

    @pl.when((k == nk - 1) & (j == 0))
    def _():
        finish(True)

    @pl.when((k == nk - 1) & (j > 0))
    def _():
        finish(False)


def matmul_rowres(a, b, res, *, emit_stats, tm, tn, tk, name):
    m, kdim = a.shape
    n_tiles, _, tile_cols = b.shape
    n = n_tiles * tile_cols
    n_split = tn // tile_cols
    nk = kdim // tk
    assert m % tm == 0 and n % tn == 0 and tn % tile_cols == 0 and kdim % tk == 0 and nk >= 2
    last = lambda i, k, j: (i, jnp.where(k == nk - 1, j, 0))
    tile_f32 = tm * tn * 4
    vmem_bytes = (n // tn + 5) * tile_f32 + (tm * tk + 2 * tk * tn) * 2
    out_specs = [pl.BlockSpec((tm, tn), last)]
    out_shape = [jax.ShapeDtypeStruct((m, n), F32)]
    scratch = [pltpu.VMEM((n // tn, tm, tn), F32)]
    if emit_stats:
        out_specs += [pl.BlockSpec((tm, tn), last), pl.BlockSpec((tm, LANES), lambda i, k, j: (i, 0))]
        out_shape += [jax.ShapeDtypeStruct((m, n), BF16), jax.ShapeDtypeStruct((m, LANES), F32)]
        scratch.append(pltpu.VMEM((tm, LANES), F32))
        vmem_bytes += 2 * tm * tn * 2 + 3 * tm * LANES * 4
    assert vmem_bytes + VMEM_RESERVE_BYTES <= VMEM_CAPACITY_BYTES
    outs = pl.pallas_call(
        functools.partial(_mm_rowres_body, nk=nk, n_split=n_split, emit_stats=emit_stats),
        grid=(m // tm, nk, n // tn),
        in_specs=([pl.BlockSpec((tm, tk), lambda i, k, j: (i, k), pipeline_mode=pl.Buffered(1))]
                  + [pl.BlockSpec((None, tk, tile_cols), lambda i, k, j, s=s: (n_split * j + s, k, 0))
                     for s in range(n_split)]
                  + [pl.BlockSpec((tm, tn), last)]),
        out_specs=tuple(out_specs),
        out_shape=tuple(out_shape),
        scratch_shapes=scratch,
        compiler_params=_cparams(("parallel", "arbitrary", "arbitrary"),
                                 max(VMEM_LIMIT_BYTES, vmem_bytes + VMEM_RESERVE_BYTES)),
        name=name,
    )(a, *([b] * n_split), res)
    return outs if emit_stats else outs[0]


def _epi_id(accs, res):
    return accs[0]


def _epi_gelu(accs, res):
    return jax.nn.gelu(accs[0])


def _epi_res(accs, res):
    return res + accs[0]


def _epi_swiglu(accs, res):
    return jax.nn.silu(accs[0]) * accs[1]


def _epi_glu_res(accs, res):
    return res + accs[0] * jax.nn.sigmoid(accs[1])


def _gmlp_gate_body(u_ref, v_ref, gain_ref, ws_ref, bs_ref, o_ref, *, n_chunks, group_dim):
    v = v_ref[...].astype(F32)
    v = v - jnp.mean(v, axis=-1, keepdims=True)
    v = v * lax.rsqrt(jnp.mean(v * v, axis=-1, keepdims=True) + LN_EPS)
    vn = (v * gain_ref[...]).astype(BF16)
    row = lax.broadcasted_iota(jnp.int32, (A_CHUNK, A_CHUNK), 0)
    col = lax.broadcasted_iota(jnp.int32, (A_CHUNK, A_CHUNK), 1)
    causal = col <= row
    n_groups = ws_ref.shape[0]
    for g in range(n_groups):
        w = jnp.where(causal, ws_ref[g], 0.0).astype(BF16)
        bias = bs_ref[:, g:g + 1]
        cols = slice(g * group_dim, (g + 1) * group_dim)
        for c in range(n_chunks):
            rows = slice(c * A_CHUNK, (c + 1) * A_CHUNK)
            mixed = jnp.dot(w, vn[rows, cols], preferred_element_type=F32) + bias
            o_ref[rows, cols] = (u_ref[rows, cols].astype(F32) * mixed).astype(o_ref.dtype)


def gmlp_gate(uv, v_gain, w_spatial, b_spatial, n_chunks=2):
    t, w2 = uv.shape
    width = w2 // 2
    tm = n_chunks * A_CHUNK
    n_groups = w_spatial.shape[0]
    return pl.pallas_call(
        functools.partial(_gmlp_gate_body, n_chunks=n_chunks, group_dim=width // n_groups),
        grid=(t // tm,),
        in_specs=[pl.BlockSpec((tm, width), lambda i: (i, 0)),
                  pl.BlockSpec((tm, width), lambda i: (i, 1)),
                  pl.BlockSpec((1, width), lambda i: (0, 0)),
                  pl.BlockSpec((n_groups, A_CHUNK, A_CHUNK), lambda i: (0, 0, 0)),
                  pl.BlockSpec((A_CHUNK, n_groups), lambda i: (0, 0))],
        out_specs=pl.BlockSpec((tm, width), lambda i: (i, 0)),
        out_shape=jax.ShapeDtypeStruct((t, width), BF16),
        compiler_params=_cparams(("parallel",)),
        name="gmlp_gate",
    )(uv, uv, v_gain.reshape(1, width), w_spatial, b_spatial.T)


def _s5_discretize_body(lr_ref, li_ref, ls_ref, br_ref, bi_ref, abr_ref, abi_ref, bbr_ref, bbi_ref):
    lr, li = lr_ref[...], li_ref[...]
    dt = jnp.exp(ls_ref[...])
    decay = jnp.exp(lr * dt)
    ab_re, ab_im = decay * jnp.cos(li * dt), decay * jnp.sin(li * dt)
    den = lr * lr + li * li
    nr, ni = ab_re - 1.0, ab_im
    coef_re = (nr * lr + ni * li) / den
    coef_im = (ni * lr - nr * li) / den
    br, bi = br_ref[...], bi_ref[...]
    abr_ref[...] = ab_re
    abi_ref[...] = ab_im
    bbr_ref[...] = coef_re * br - coef_im * bi
    bbi_ref[...] = coef_re * bi + coef_im * br


def s5_discretize(lam_re, lam_im, log_step, b_re, b_im):
    g, p = lam_re.shape
    rows = g * B_GROUP_CH
    rep = lambda z: jnp.repeat(z, B_GROUP_CH, axis=0)
    b_rows = lambda b: jnp.transpose(b, (0, 2, 1)).reshape(rows, p)
    out = jax.ShapeDtypeStruct((rows, p), F32)
    return pl.pallas_call(
        _s5_discretize_body,
        out_shape=(out, out, out, out),
        name="s5_discretize",
    )(rep(lam_re), rep(lam_im), rep(log_step.reshape(g, 1)), b_rows(b_re), b_rows(b_im))


def _s5_scan_body(u_ref, bblk_ref, cre_ref, cim_ref, ar_ref, ai_ref, d_ref, y_ref,
                  lhs_ref, bu_ref, x_ref, yl_ref, *, steps, n_state):
    ls = S5_CHUNK_STEPS
    n_chunks = steps // ls
    ar = jnp.broadcast_to(ar_ref[...], (S5_SEGMENTS, n_state))
    ai = jnp.broadcast_to(ai_ref[...], (S5_SEGMENTS, n_state))

    rows_chunk = ls * S5_SEGMENTS

    def chunk_rows(c):
        return pl.ds(pl.multiple_of(c * rows_chunk, rows_chunk), rows_chunk)

    def project(c, gather_rows):
        if gather_rows:
            def gather(i, carry):
                lhs_ref[pl.ds(c * rows_chunk + i * S5_SEGMENTS, S5_SEGMENTS), :] = (
                    u_ref[pl.ds(c * ls + i, S5_SEGMENTS, stride=steps), :])
                return carry
            lax.fori_loop(0, ls, gather, 0, unroll=8)
        bu_ref[...] = jnp.dot(lhs_ref[chunk_rows(c), :].astype(BF16), bblk_ref[...],
                              preferred_element_type=F32)

    def recur(state, store):
        def step(i, carry):
            xr, xi = carry
            rows = pl.ds(i * S5_SEGMENTS, S5_SEGMENTS)
            nxr = ar * xr - ai * xi + bu_ref[rows, 0:n_state]
            nxi = ar * xi + ai * xr + bu_ref[rows, n_state:2 * n_state]
            if store:
                x_ref[rows, 0:n_state] = nxr
                x_ref[rows, n_state:2 * n_state] = nxi
            return nxr, nxi
        return lax.fori_loop(0, ls, step, state, unroll=2)

    zero = jnp.zeros((S5_SEGMENTS, n_state), F32)

    def pass1(c, state):
        project(c, True)
        return recur(state, False)
    er, ei = lax.fori_loop(0, n_chunks, pass1, (zero, zero))

    pr, pi = ar_ref[...], ai_ref[...]
    for _ in range(int(math.log2(steps))):
        pr, pi = pr * pr - pi * pi, 2.0 * pr * pi
    cr = jnp.zeros((1, n_state), F32)
    ci = jnp.zeros((1, n_state), F32)
    init_r, init_i = [], []
    for j in range(S5_SEGMENTS):
        init_r.append(cr)
        init_i.append(ci)
        cr, ci = (er[j:j + 1] + pr * cr - pi * ci, ei[j:j + 1] + pr * ci + pi * cr)
    x0 = (jnp.concatenate(init_r, axis=0), jnp.concatenate(init_i, axis=0))

    d = d_ref[...]

    def pass2(c, state):
        project(c, False)
        state = recur(state, True)
        xr = x_ref[:, 0:n_state].astype(BF16)
        xi = x_ref[:, n_state:2 * n_state].astype(BF16)
        y = (jnp.dot(xr, cre_ref[...], preferred_element_type=F32)
             - jnp.dot(xi, cim_ref[...], preferred_element_type=F32))
        yl_ref[...] = jax.nn.gelu(y + d * lhs_ref[chunk_rows(c), :])

        def emit(i, carry):
            y_ref[pl.ds(c * ls + i, S5_SEGMENTS, stride=steps), :] = (
                yl_ref[pl.ds(i * S5_SEGMENTS, S5_SEGMENTS), :])
            return carry
        lax.fori_loop(0, ls, emit, 0, unroll=8)
        return state
    lax.fori_loop(0, n_chunks, pass2, x0)


def s5_scan(u, ab_re, ab_im, bb_re, bb_im, c_re, c_im, d_skip):
    t, width = u.shape
    g, p = ab_re.shape
    gb = S5_GROUPS_PER_BLOCK
    n_blocks = g // gb
    n_state = gb * p
    steps = t // S5_SEGMENTS
    assert steps % S5_CHUNK_STEPS == 0 and steps & (steps - 1) == 0
    eye = jnp.eye(gb, dtype=F32)

    def blockdiag_in(bb):
        z = bb.reshape(n_blocks, gb, B_GROUP_CH, p)
        return jnp.einsum('bgcp,gh->bgchp', z, eye).reshape(n_blocks, gb * B_GROUP_CH, n_state)

    def blockdiag_out(cc):
        z = cc.reshape(n_blocks, gb, B_GROUP_CH, p)
        return jnp.einsum('bgcp,gh->bhpgc', z, eye).reshape(n_blocks, n_state, gb * B_GROUP_CH)

    bblk = jnp.concatenate([blockdiag_in(bb_re), blockdiag_in(bb_im)], axis=-1).astype(BF16)
    cre = blockdiag_out(c_re).astype(BF16)
    cim = blockdiag_out(c_im).astype(BF16)
    ar = ab_re.reshape(n_blocks, 1, n_state)
    ai = ab_im.reshape(n_blocks, 1, n_state)
    d = d_skip.reshape(n_blocks, 1, LANES)
    rows_chunk = S5_CHUNK_STEPS * S5_SEGMENTS
    return pl.pallas_call(
        functools.partial(_s5_scan_body, steps=steps, n_state=n_state),
        grid=(n_blocks,),
        in_specs=[pl.BlockSpec((t, LANES), lambda b: (0, b)),
                  pl.BlockSpec((None, LANES, 2 * n_state), lambda b: (b, 0, 0)),
                  pl.BlockSpec((None, n_state, LANES), lambda b: (b, 0, 0)),
                  pl.BlockSpec((None, n_state, LANES), lambda b: (b, 0, 0)),
                  pl.BlockSpec((None, 1, n_state), lambda b: (b, 0, 0)),
                  pl.BlockSpec((None, 1, n_state), lambda b: (b, 0, 0)),
                  pl.BlockSpec((None, 1, LANES), lambda b: (b, 0, 0))],
        out_specs=pl.BlockSpec((t, LANES), lambda b: (0, b)),
        out_shape=jax.ShapeDtypeStruct((t, width), F32),
        scratch_shapes=[pltpu.VMEM((t, LANES), F32),
                        pltpu.VMEM((rows_chunk, 2 * n_state), F32),
                        pltpu.VMEM((rows_chunk, 2 * n_state), F32),
                        pltpu.VMEM((rows_chunk, LANES), F32)],
        compiler_params=_cparams(("parallel",)),
        name="s5_scan",
    )(u, bblk, cre, cim, ar, ai, d)


def _attn_body(q_ref, kp_ref, kc_ref, vp_ref, vc_ref, o_ref, lse_ref, *, blocks_per_residue, n_heads):
    b = pl.program_id(0)
    not_first = (b % blocks_per_residue) != 0
    qi = lax.broadcasted_iota(jnp.int32, (C_QBLOCK, 2 * C_QBLOCK), 0)
    kj = lax.broadcasted_iota(jnp.int32, (C_QBLOCK, 2 * C_QBLOCK), 1)
    dist = C_QBLOCK + qi - kj
    valid = (dist >= 0) & (dist <= C_QBLOCK) & (not_first | (kj >= C_QBLOCK))
    scale = C_HEAD_DIM ** -0.5
    for h in range(n_heads):
        cols = slice(h * C_HEAD_DIM, (h + 1) * C_HEAD_DIM)
        q = q_ref[:, cols]
        k = jnp.concatenate([kp_ref[:, cols], kc_ref[:, cols]], axis=0)
        v = jnp.concatenate([vp_ref[:, cols], vc_ref[:, cols]], axis=0)
        s = lax.dot_general(q, k, (((1,), (1,)), ((), ())), preferred_element_type=F32) * scale
        s = jnp.where(valid, s, -jnp.inf)
        m = jnp.max(s, axis=-1, keepdims=True)
        e = jnp.exp(s - m)
        ssum = jnp.sum(e, axis=-1, keepdims=True)
        p = e / ssum
        o_ref[:, cols] = jnp.dot(p.astype(BF16), v, preferred_element_type=F32).astype(o_ref.dtype)
        lse_ref[:, h:h + 1] = m + jnp.log(ssum)


def window_attention(qkv, dil):
    t, w3 = qkv.shape
    width = w3 // 3
    n_heads = width // C_HEAD_DIM
    n_blocks = t // C_QBLOCK
    cur = lambda part: pl.BlockSpec((C_QBLOCK, width), lambda b, p=part: (b, p))
    prev = lambda part: pl.BlockSpec((C_QBLOCK, width), lambda b, p=part: (jnp.maximum(b - 1, 0), p))
    return pl.pallas_call(
        functools.partial(_attn_body, blocks_per_residue=n_blocks // dil, n_heads=n_heads),
        grid=(n_blocks,),
        in_specs=[cur(0), prev(1), cur(1), prev(2), cur(2)],
        out_specs=(pl.BlockSpec((C_QBLOCK, width), lambda b: (b, 0)),
                   pl.BlockSpec((C_QBLOCK, n_heads), lambda b: (b, 0))),
        out_shape=(jax.ShapeDtypeStruct((t, width), BF16),
                   jax.ShapeDtypeStruct((t, n_heads), F32)),
        compiler_params=_cparams(("parallel",)),
        name=f"window_attention_d{dil}",
    )(qkv, qkv, qkv, qkv, qkv)


def _residue_permutation(n, dil, inverse=False):
    row = lax.broadcasted_iota(jnp.int32, (n, n), 0)
    col = lax.broadcasted_iota(jnp.int32, (n, n), 1)
    per = n // dil
    if inverse:
        hit = col == (row % dil) * per + row // dil
    else:
        hit = col == (row % per) * dil + row // per
    return jnp.where(hit, 1.0, 0.0).astype(BF16)


def _residue_copies_body(x_ref, *o_refs, dils):
    x = x_ref[...]
    tm = x.shape[0]
    for dil, o_ref in zip(dils, o_refs):
        per = tm // dil
        shuffled = jnp.dot(_residue_permutation(tm, dil), x, preferred_element_type=F32).astype(BF16)
        for r in range(dil):
            o_ref[r] = shuffled[r * per:(r + 1) * per]


def residue_major_copies(x, dils, tm=256):
    t, d = x.shape
    outs = pl.pallas_call(
        functools.partial(_residue_copies_body, dils=dils),
        grid=(t // tm,),
        in_specs=[pl.BlockSpec((tm, d), lambda i: (i, 0))],
        out_specs=tuple(pl.BlockSpec((dil, tm // dil, d), lambda i: (0, i, 0)) for dil in dils),
        out_shape=tuple(jax.ShapeDtypeStruct((dil, t // dil, d), BF16) for dil in dils),
        compiler_params=_cparams(("parallel",)),
        name="residue_major_copies",
    )(x)
    return [o.reshape(t, d) for o in outs]


def _attn_combine_body(*refs, dils):
    n = len(dils)
    o_refs, l_refs, out_ref = refs[:n], refs[n:2 * n], refs[2 * n]
    scr_refs = refs[2 * n + 1:]
    tm, width = out_ref.shape
    nat = []
    scr = iter(scr_refs)
    for dil, o_ref in zip(dils, o_refs):
        if dil == 1:
            nat.append(o_ref.at[0])
            continue
        s_ref = next(scr)
        o2d = jnp.concatenate([o_ref[r] for r in range(dil)], axis=0)
        s_ref[...] = jnp.dot(_residue_permutation(tm, dil, inverse=True), o2d,
                             preferred_element_type=F32)
        nat.append(s_ref)
    ls = [l[...] for l in l_refs]
    mx = functools.reduce(jnp.maximum, ls)
    es = [jnp.exp(l - mx) for l in ls]
    den = functools.reduce(lambda a, b: a + b, es)
    alphas = [e / den for e in es]
    n_heads = ls[0].shape[1]
    for h in range(n_heads):
        cols = slice(h * C_HEAD_DIM, (h + 1) * C_HEAD_DIM)
        acc = None
        for alpha, o in zip(alphas, nat):
            term = alpha[:, h:h + 1] * o[:, cols].astype(F32)
            acc = term if acc is None else acc + term
        out_ref[:, cols] = acc.astype(out_ref.dtype)


def attn_combine(outs, lses, dils, tm=256):
    t, width = outs[0].shape
    n_heads = lses[0].shape[1]
    o_specs = [pl.BlockSpec((dil, tm // dil, width), lambda i: (0, i, 0)) for dil in dils]
    l_spec = pl.BlockSpec((tm, n_heads), lambda i: (i, 0))
    return pl.pallas_call(
        functools.partial(_attn_combine_body, dils=dils),
        grid=(t // tm,),
        in_specs=o_specs + [l_spec] * len(dils),
        out_specs=pl.BlockSpec((tm, width), lambda i: (i, 0)),
        out_shape=jax.ShapeDtypeStruct((t, width), BF16),
        scratch_shapes=[pltpu.VMEM((tm, width), F32) for dil in dils if dil > 1],
        compiler_params=_cparams(("parallel",)),
        name="attn_combine",
    )(*[o.reshape(dil, t // dil, width) for o, dil in zip(outs, dils)], *lses)


def _from_residue_major(a, dil):
    t, d = a.shape
    return a.reshape(dil, t // dil, d).transpose(1, 0, 2).reshape(t, d)


def _to_residue_major(a, dil):
    t, d = a.shape
    return a.reshape(t // dil, dil, d).transpose(1, 0, 2).reshape(t, d)


def ffn_layer(stream, gain, w_gate, w_up, w_down, layer, emit_stats):
    h, hb, ssq = stream
    ff = w_gate.shape[2]
    mid, w_down_bf16 = matmul_ws(hb, [w_gate, w_up], layer, [0, 0], ff, _epi_swiglu, BF16,
                                 norm=(ssq, gain), side=(w_down, layer, 256), tm=1024, tn=512,
                                 name="ffn_gate_up")
    out = matmul_rowres(mid, w_down_bf16, h, emit_stats=emit_stats, tm=1024, tn=512, tk=ff // 2,
                        name="ffn_down")
    return out if emit_stats else (out, None, None)


def gmlp_layer(stream, gain, w_in, v_gain, w_spatial, b_spatial, w_out, layer):
    h, hb, ssq = stream
    d = h.shape[1]
    uv = matmul_ws(hb, [w_in], layer, [0], w_in.shape[2], _epi_gelu, BF16, norm=(ssq, gain),
                   tm=1024, tn=1024, name="gmlp_in")
    gated = gmlp_gate(uv, v_gain, w_spatial, b_spatial)
    return matmul_ws(gated, [w_out], layer, [0], d, _epi_res, F32, res=h, emit_stats=True,
                     tm=512, tn=1024, name="gmlp_out")


def s5_layer(stream, gain, w_in, lam_re, lam_im, log_step, b_re, b_im, c_re, c_im, d_skip, w_out,
             layer):
    h, hb, ssq = stream
    d = h.shape[1]
    u = matmul_ws(hb, [w_in], layer, [0], w_in.shape[2], _epi_id, F32, norm=(ssq, gain), tm=1024,
                  tn=min(1024, w_in.shape[2]), name="s5_in")
    ab_re, ab_im, bb_re, bb_im = s5_discretize(lam_re, lam_im, log_step, b_re, b_im)
    y = s5_scan(u, ab_re[::B_GROUP_CH], ab_im[::B_GROUP_CH], bb_re, bb_im, c_re, c_im, d_skip)
    return matmul_ws(y.astype(BF16), [w_out, w_out], layer, [0, d], d, _epi_glu_res, F32, res=h,
                     emit_stats=True, tm=1024, tn=1024, name="s5_out")


def attn_layer(stream, gain, w_in, w_out, layer):
    h, hb, ssq = stream
    d = h.shape[1]
    width = w_out.shape[1]
    dils = tuple(dil for _, dil in C_PATTERNS)
    assert all(window // dil == C_QBLOCK for window, dil in C_PATTERNS) and dils[0] == 1
    hbs = [hb] + residue_major_copies(hb, dils[1:])
    tn = 1024 if (3 * width) % 1024 == 0 else 512
    outs, lses = [], []
    for p_idx, dil in enumerate(dils):
        qkv = matmul_ws(hbs[p_idx], [w_in], layer, [p_idx * 3 * width], 3 * width, _epi_id, BF16,
                        norm=(_to_residue_major(ssq, dil), gain), tm=1024, tn=tn,
                        name=f"attn_in_d{dil}")
        o, lse = window_attention(qkv, dil)
        outs.append(o)
        lses.append(_from_residue_major(lse, dil))
    mixed = attn_combine(outs, lses, dils)
    return matmul_ws(mixed, [w_out], layer, [0], d, _epi_res, F32, res=h, emit_stats=True,
                     tm=1024, tn=1024, name="attn_out")


def kernel(x, norm_mix, norm_ffn, w_gate, w_up, w_down, a_w_in, a_v_gain, a_w_spatial, a_b_spatial,
           a_w_out, b_w_in, b_lambda_re, b_lambda_im, b_log_step, b_b_re, b_b_im, b_c_re, b_c_im,
           b_d_skip, b_w_out, c_w_in, c_w_out, final_norm):
    bsz, seq, d = x.shape
    depth = norm_mix.shape[0]
    outs = []
    for bi in range(bsz):
        stream = (x[bi],) + tuple(stream_stats(x[bi]))
        for i in range(depth):
            kind, j = i % 3, i // 3
            if kind == 0:
                stream = gmlp_layer(stream, norm_mix[i], a_w_in, a_v_gain[j], a_w_spatial[j],
                                    a_b_spatial[j], a_w_out, j)
            elif kind == 1:
                stream = s5_layer(stream, norm_mix[i], b_w_in, b_lambda_re[j], b_lambda_im[j],
                                  b_log_step[j], b_b_re[j], b_b_im[j], b_c_re[j], b_c_im[j],
                                  b_d_skip[j], b_w_out, j)
            else:
                stream = attn_layer(stream, norm_mix[i], c_w_in, c_w_out, j)
            stream = ffn_layer(stream, norm_ffn[i], w_gate, w_up, w_down, i,
                               emit_stats=i + 1 < depth)
        outs.append(rmsnorm(stream[0], final_norm, x.dtype))
    return jnp.stack(outs)
```

```python
import functools
import math

import jax
import jax.numpy as jnp
from jax import lax
from jax.experimental import pallas as pl
from jax.experimental.pallas import tpu as pltpu

F32 = jnp.float32
BF16 = jnp.bfloat16

RMS_EPS = 1e-6
LN_EPS = 1e-5

LANES = 128
SUBLANES = 8
BF16_SUBLANES = 16
VMEM_CAPACITY_BYTES = 64 * 1024 * 1024
VMEM_RESERVE_BYTES = 3 * 1024 * 1024
VMEM_LIMIT_BYTES = 56 * 1024 * 1024

A_CHUNK = 128
A_GROUPS = 8
B_GROUP_CH = 16
S5_GROUPS_PER_BLOCK = LANES // B_GROUP_CH
S5_SEGMENTS = SUBLANES
S5_CHUNK_STEPS = 128
C_HEAD_DIM = 128
C_QBLOCK = 128
C_PATTERNS = ((128, 1), (512, 4), (2048, 16))


def _cparams(sem, vmem_limit_bytes=VMEM_LIMIT_BYTES):
    return pltpu.CompilerParams(dimension_semantics=sem, vmem_limit_bytes=vmem_limit_bytes)


def _rmsnorm_body(x_ref, g_ref, o_ref):
    x = x_ref[...]
    y = x * lax.rsqrt(jnp.mean(x * x, axis=-1, keepdims=True) + RMS_EPS)
    o_ref[...] = (y * g_ref[...]).astype(o_ref.dtype)


def rmsnorm(x, gain, out_dtype, tm=256):
    t, d = x.shape
    return pl.pallas_call(
        _rmsnorm_body,
        grid=(t // tm,),
        in_specs=[pl.BlockSpec((tm, d), lambda i: (i, 0)),
                  pl.BlockSpec((1, d), lambda i: (0, 0))],
        out_specs=pl.BlockSpec((tm, d), lambda i: (i, 0)),
        out_shape=jax.ShapeDtypeStruct((t, d), out_dtype),
        compiler_params=_cparams(("parallel",)),
        name="rmsnorm",
    )(x, gain.reshape(1, d))


def _lane_group_sum(x):
    out = x[:, 0:LANES]
    for g in range(1, x.shape[1] // LANES):
        out = out + x[:, g * LANES:(g + 1) * LANES]
    return out


def _rms_row_scale(ssq_ref, d_model):
    return lax.rsqrt(jnp.sum(ssq_ref[...], axis=-1, keepdims=True) / d_model + RMS_EPS)


def _stream_stats_body(x_ref, hb_ref, ssq_ref):
    x = x_ref[...]
    hb_ref[...] = x.astype(BF16)
    ssq_ref[...] = _lane_group_sum(x * x)


def stream_stats(x, tm=256):
    t, d = x.shape
    return pl.pallas_call(
        _stream_stats_body,
        grid=(t // tm,),
        in_specs=[pl.BlockSpec((tm, d), lambda i: (i, 0))],
        out_specs=(pl.BlockSpec((tm, d), lambda i: (i, 0)),
                   pl.BlockSpec((tm, LANES), lambda i: (i, 0))),
        out_shape=(jax.ShapeDtypeStruct((t, d), BF16), jax.ShapeDtypeStruct((t, LANES), F32)),
        compiler_params=_cparams(("parallel",)),
        name="stream_stats",
    )(x)


def _mm_ws_body(*refs, n_b, has_res, has_norm, emit_stats, has_side, nj, sub, last_cols, d_model,
                epilogue):
    it = iter(refs)
    a_ref = next(it)
    w_refs = [next(it) for _ in range(n_b)]
    res_ref = next(it) if has_res else None
    ssq_ref, gain_ref = (next(it), next(it)) if has_norm else (None, None)
    side_ref = next(it) if has_side else None
    o_ref = next(it)
    hb_ref, ssq_out_ref = (next(it), next(it)) if emit_stats else (None, None)
    side_out_ref = next(it) if has_side else None
    wbf_refs = [(next(it), next(it)) for _ in range(n_b)]
    ssq_acc_ref = next(it) if emit_stats else None
    jj = pl.program_id(0)
    i = pl.program_id(1)
    kc = w_refs[0].shape[0]

    def cast_into(buf):
        rows = pl.ds(pl.multiple_of(i * kc, kc), kc)
        for w_ref, wbf in zip(w_refs, wbf_refs):
            w = w_ref[...]
            if has_norm:
                w = w * gain_ref[...]
            wbf[buf][rows, :] = w.astype(BF16)

    def side_cast():
        if has_side:
            n_tiles, _, tile_cols = side_out_ref.shape
            for t in range(n_tiles):
                side_out_ref[t] = side_ref[:, t * tile_cols:(t + 1) * tile_cols].astype(BF16)

    def compute_from(buf, n_cols):
        a = a_ref[...]
        rs = _rms_row_scale(ssq_ref, d_model) if has_norm else None
        total = None
        for c in range(pl.cdiv(n_cols, sub)):
            cols = slice(c * sub, (c + 1) * sub)
            accs = [jnp.dot(a, wbf[buf][:, cols], preferred_element_type=F32) for wbf in wbf_refs]
            if has_norm:
                accs = [acc * rs for acc in accs]
            out = epilogue(accs, res_ref[:, cols] if has_res else None)
            o_ref[:, cols] = out.astype(o_ref.dtype)
            if emit_stats:
                hb_ref[:, cols] = out.astype(BF16)
                part = _lane_group_sum(out * out)
                total = part if total is None else total + part
        return total

    def add_stats(total, first, last=False):
        if emit_stats:
            total = total if first else ssq_acc_ref[i] + total
            if last:
                ssq_out_ref[...] = total
            else:
                ssq_acc_ref[i] = total

    @pl.when(jj == 0)
    def _():
        cast_into(0)
        side_cast()

    tn = o_ref.shape[1]
    if nj > 1:
        @pl.when(jj == 1)
        def _():
            total = compute_from(0, tn)
            cast_into(1)
            side_cast()
            add_stats(total, True)

    for parity in (0, 1):
        @pl.when((jj > 1) & (jj < nj) & (jj % 2 == parity))
        def _():
            total = compute_from(1 - parity, tn)
            cast_into(parity)
            side_cast()
            add_stats(total, False)

    @pl.when(jj == nj)
    def _():
        total = compute_from((nj - 1) % 2, last_cols)
        side_cast()
        add_stats(total, nj == 1, last=True)


def matmul_ws(a, ws, layer, col_offsets, n_out, epilogue, out_dtype, *, res=None, norm=None,
              emit_stats=False, side=None, tm, tn, sub=256, name):
    m, kdim = a.shape
    ni = m // tm
    nj = pl.cdiv(n_out, tn)
    kc = kdim // ni
    assert m % tm == 0 and kdim % ni == 0 and kc % SUBLANES == 0
    assert all(off % tn == 0 for off in col_offsets)
    assert not emit_stats or (res is not None and out_dtype == F32 and n_out % tn == 0)
    row = lambda jj, i: jnp.where(jj == 0, 0, i)
    in_specs = [pl.BlockSpec((tm, kdim), lambda jj, i: (row(jj, i), 0))]
    for off in col_offsets:
        in_specs.append(pl.BlockSpec(
            (None, kc, tn), lambda jj, i, o=off // tn: (layer, i, jnp.minimum(jj, nj - 1) + o)))
    args = [a] + list(ws)
    out_spec = pl.BlockSpec((tm, tn), lambda jj, i: (row(jj, i), jnp.maximum(jj - 1, 0)))
    stat_spec = pl.BlockSpec((tm, LANES), lambda jj, i: (row(jj, i), 0))
    if res is not None:
        in_specs.append(out_spec)
        args.append(res)
    if norm is not None:
        ssq, gain = norm
        in_specs += [stat_spec, pl.BlockSpec((kc, 1), lambda jj, i: (i, 0))]
        args += [ssq, gain.reshape(kdim, 1)]
    out_specs = [out_spec]
    out_shape = [jax.ShapeDtypeStruct((m, n_out), out_dtype)]
    scratch = [pltpu.VMEM((kdim, tn), BF16) for _ in ws for _ in range(2)]
    if emit_stats:
        out_specs += [out_spec, pl.BlockSpec((tm, LANES), lambda jj, i: (jnp.where(jj == nj, i, 0), 0))]
        out_shape += [jax.ShapeDtypeStruct((m, n_out), BF16), jax.ShapeDtypeStruct((m, LANES), F32)]
        scratch.append(pltpu.VMEM((ni, tm, LANES), F32))
    if side is not None:
        stack, side_layer, tile_cols = side
        _, s_rows, s_cols = stack.shape
        assert s_cols % tile_cols == 0
        sr = next(r for r in range(BF16_SUBLANES, s_rows + 1, BF16_SUBLANES)
                  if s_rows % r == 0 and s_rows // r <= (nj + 1) * ni)
        n_side = s_rows // sr
        blk = lambda jj, i: jnp.minimum(jj * ni + i, n_side - 1)
        in_specs.append(pl.BlockSpec((None, sr, s_cols), lambda jj, i: (side_layer, blk(jj, i), 0)))
        args.append(stack)
        n_tiles = s_cols // tile_cols
        out_specs.append(pl.BlockSpec((n_tiles, sr, tile_cols), lambda jj, i: (0, blk(jj, i), 0)))
        out_shape.append(jax.ShapeDtypeStruct((n_tiles, s_rows, tile_cols), BF16))
    stat_tile = tm * LANES * 4
    vmem_bytes = (2 * tm * kdim * 2 + len(ws) * (2 * kdim * tn * 2 + 2 * kc * tn * 4)
                  + 2 * tm * tn * jnp.dtype(out_dtype).itemsize + 2 * len(ws) * tm * sub * 4)
    if res is not None:
        vmem_bytes += 2 * tm * tn * 4
    if norm is not None:
        vmem_bytes += 2 * stat_tile + 2 * kc * LANES * 4
    if emit_stats:
        vmem_bytes += 2 * tm * tn * 2 + (2 + ni) * stat_tile
    if side is not None:
        vmem_bytes += 2 * sr * s_cols * (4 + 2)
    assert vmem_bytes + VMEM_RESERVE_BYTES <= VMEM_CAPACITY_BYTES
    outs = pl.pallas_call(
        functools.partial(_mm_ws_body, n_b=len(ws), has_res=res is not None, has_norm=norm is not None,
                          emit_stats=emit_stats, has_side=side is not None, nj=nj, sub=min(sub, tn),
                          last_cols=n_out - (nj - 1) * tn, d_model=kdim, epilogue=epilogue),
        grid=(nj + 1, ni),
        in_specs=in_specs,
        out_specs=tuple(out_specs),
        out_shape=tuple(out_shape),
        scratch_shapes=scratch,
        compiler_params=_cparams(("arbitrary", "arbitrary"),
                                 max(VMEM_LIMIT_BYTES, vmem_bytes + VMEM_RESERVE_BYTES)),
        name=name,
    )(*args)
    return outs[0] if len(outs) == 1 else outs


def _mm_rowres_body(*refs, nk, n_split, n_prev, emit_stats):
    it = iter(refs)
    a_ref = next(it)
    b_refs = [next(it) for _ in range(n_split)]
    res_ref = next(it)
    prev_ssq_ref = next(it) if emit_stats and n_prev else None
    for _ in range(n_prev):
        next(it)
    o_ref = next(it)
    hb_ref, ssq_out_ref = (next(it), next(it)) if emit_stats else (None, None)
    acc_ref = next(it)
    ssq_acc_ref = next(it) if emit_stats else None
    k = pl.program_id(1)
    j = pl.program_id(2)
    sub = b_refs[0].shape[1]

    def accumulate(first):
        a = a_ref[...]
        for c, b_ref in enumerate(b_refs):
            cols = slice(c * sub, (c + 1) * sub)
            part = jnp.dot(a, b_ref[...], preferred_element_type=F32)
            acc_ref[j, :, cols] = part if first else acc_ref[j, :, cols] + part

    @pl.when(k == 0)
    def _():
        accumulate(True)

    @pl.when((k > 0) & (k < nk - 1))
    def _():
        accumulate(False)

    def finish(first):
        a = a_ref[...]
        total = None
        for c, b_ref in enumerate(b_refs):
            cols = slice(c * sub, (c + 1) * sub)
            out = (res_ref[:, cols] + acc_ref[j, :, cols]
                   + jnp.dot(a, b_ref[...], preferred_element_type=F32))
            o_ref[:, cols] = out
            if emit_stats:
                hb_ref[:, cols] = out.astype(BF16)
                part = _lane_group_sum(out * out)
                total = part if total is None else total + part
        if emit_stats:
            if not first:
                total = ssq_acc_ref[...] + total
            elif prev_ssq_ref is not None:
                total = prev_ssq_ref[...] + total
            ssq_acc_ref[...] = total
            ssq_out_ref[...] = total

    @pl.when((k == nk - 1) & (j == 0))
    def _():
        finish(True)

    @pl.when((k == nk - 1) & (j > 0))
    def _():
        finish(False)


def matmul_rowres(a, b, res, *, col_tiles, prev=None, emit_stats, tm, tn, tk, name):
    m, kdim = a.shape
    n_tiles, _, tile_cols = b.shape
    n = n_tiles * tile_cols
    n_split = tn // tile_cols
    nk = kdim // tk
    start, count = col_tiles
    assert m % tm == 0 and n % tn == 0 and tn % tile_cols == 0 and kdim % tk == 0 and nk >= 2
    assert 0 <= start and start + count <= n // tn
    last = lambda i, k, j: (i, start + jnp.where(k == nk - 1, j, 0))
    stat = lambda i, k, j: (i, 0)
    tile_f32 = tm * tn * 4
    vmem_bytes = (count + 5) * tile_f32 + 2 * (tm * tk + tk * tn) * 2
    in_specs = ([pl.BlockSpec((tm, tk), lambda i, k, j: (i, k))]
                + [pl.BlockSpec((None, tk, tile_cols),
                                lambda i, k, j, s=s: (n_split * (start + j) + s, k, 0))
                   for s in range(n_split)]
                + [pl.BlockSpec((tm, tn), last)])
    args = [a] + [b] * n_split + [res]
    out_specs = [pl.BlockSpec((tm, tn), last)]
    out_shape = [jax.ShapeDtypeStruct((m, n), F32)]
    scratch = [pltpu.VMEM((count, tm, tn), F32)]
    if emit_stats:
        out_specs += [pl.BlockSpec((tm, tn), last), pl.BlockSpec((tm, LANES), stat)]
        out_shape += [jax.ShapeDtypeStruct((m, n), BF16), jax.ShapeDtypeStruct((m, LANES), F32)]
        scratch.append(pltpu.VMEM((tm, LANES), F32))
        vmem_bytes += 2 * tm * tn * 2 + 5 * tm * LANES * 4
    aliases = {}
    n_prev = 0
    if prev is not None:
        if emit_stats:
            in_specs.append(pl.BlockSpec((tm, LANES), stat))
            args.append(prev[2])
        n_prev = 2 if emit_stats else 1
        for out_idx in range(n_prev):
            aliases[len(args)] = out_idx
            in_specs.append(pl.BlockSpec(memory_space=pl.ANY))
            args.append(prev[out_idx])
    assert vmem_bytes + VMEM_RESERVE_BYTES <= VMEM_CAPACITY_BYTES
    outs = pl.pallas_call(
        functools.partial(_mm_rowres_body, nk=nk, n_split=n_split, n_prev=n_prev,
                          emit_stats=emit_stats),
        grid=(m // tm, nk, count),
        in_specs=in_specs,
        out_specs=tuple(out_specs),
        out_shape=tuple(out_shape),
        scratch_shapes=scratch,
        input_output_aliases=aliases,
        compiler_params=_cparams(("parallel", "arbitrary", "arbitrary"),
                                 max(VMEM_LIMIT_BYTES, vmem_bytes + VMEM_RESERVE_BYTES)),
        name=name,
    )(*args)
    return tuple(outs)


def _epi_id(accs, res):
    return accs[0]


def _epi_gelu(accs, res):
    return jax.nn.gelu(accs[0])


def _epi_res(accs, res):
    return res + accs[0]


def _epi_swiglu(accs, res):
    return jax.nn.silu(accs[0]) * accs[1]


def _epi_glu_res(accs, res):
    return res + accs[0] * jax.nn.sigmoid(accs[1])


def _gmlp_gate_body(u_ref, v_ref, gain_ref, ws_ref, bs_ref, o_ref, *, n_chunks, group_dim):
    v = v_ref[...].astype(F32)
    v = v - jnp.mean(v, axis=-1, keepdims=True)
    v = v * lax.rsqrt(jnp.mean(v * v, axis=-1, keepdims=True) + LN_EPS)
    vn = (v * gain_ref[...]).astype(BF16)
    row = lax.broadcasted_iota(jnp.int32, (A_CHUNK, A_CHUNK), 0)
    col = lax.broadcasted_iota(jnp.int32, (A_CHUNK, A_CHUNK), 1)
    causal = col <= row
    n_groups = ws_ref.shape[0]
    for g in range(n_groups):
        w = jnp.where(causal, ws_ref[g], 0.0).astype(BF16)
        bias = bs_ref[:, g:g + 1]
        cols = slice(g * group_dim, (g + 1) * group_dim)
        for c in range(n_chunks):
            rows = slice(c * A_CHUNK, (c + 1) * A_CHUNK)
            mixed = jnp.dot(w, vn[rows, cols], preferred_element_type=F32) + bias
            o_ref[rows, cols] = (u_ref[rows, cols].astype(F32) * mixed).astype(o_ref.dtype)


def gmlp_gate(uv, v_gain, w_spatial, b_spatial, n_chunks=2):
    t, w2 = uv.shape
    width = w2 // 2
    tm = n_chunks * A_CHUNK
    n_groups = w_spatial.shape[0]
    return pl.pallas_call(
        functools.partial(_gmlp_gate_body, n_chunks=n_chunks, group_dim=width // n_groups),
        grid=(t // tm,),
        in_specs=[pl.BlockSpec((tm, width), lambda i: (i, 0)),
                  pl.BlockSpec((tm, width), lambda i: (i, 1)),
                  pl.BlockSpec((1, width), lambda i: (0, 0)),
                  pl.BlockSpec((n_groups, A_CHUNK, A_CHUNK), lambda i: (0, 0, 0)),
                  pl.BlockSpec((A_CHUNK, n_groups), lambda i: (0, 0))],
        out_specs=pl.BlockSpec((tm, width), lambda i: (i, 0)),
        out_shape=jax.ShapeDtypeStruct((t, width), BF16),
        compiler_params=_cparams(("parallel",)),
        name="gmlp_gate",
    )(uv, uv, v_gain.reshape(1, width), w_spatial, b_spatial.T)


def _s5_discretize_body(lr_ref, li_ref, ls_ref, br_ref, bi_ref, abr_ref, abi_ref, bbr_ref, bbi_ref):
    lr, li = lr_ref[...], li_ref[...]
    dt = jnp.exp(ls_ref[...])
    decay = jnp.exp(lr * dt)
    ab_re, ab_im = decay * jnp.cos(li * dt), decay * jnp.sin(li * dt)
    den = lr * lr + li * li
    nr, ni = ab_re - 1.0, ab_im
    coef_re = (nr * lr + ni * li) / den
    coef_im = (ni * lr - nr * li) / den
    br, bi = br_ref[...], bi_ref[...]
    abr_ref[...] = ab_re
    abi_ref[...] = ab_im
    bbr_ref[...] = coef_re * br - coef_im * bi
    bbi_ref[...] = coef_re * bi + coef_im * br


def s5_discretize(lam_re, lam_im, log_step, b_re, b_im):
    g, p = lam_re.shape
    rows = g * B_GROUP_CH
    rep = lambda z: jnp.repeat(z, B_GROUP_CH, axis=0)
    b_rows = lambda b: jnp.transpose(b, (0, 2, 1)).reshape(rows, p)
    out = jax.ShapeDtypeStruct((rows, p), F32)
    return pl.pallas_call(
        _s5_discretize_body,
        out_shape=(out, out, out, out),
        name="s5_discretize",
    )(rep(lam_re), rep(lam_im), rep(log_step.reshape(g, 1)), b_rows(b_re), b_rows(b_im))


def _s5_scan_body(u_ref, bblk_ref, cre_ref, cim_ref, ar_ref, ai_ref, d_ref, y_ref,
                  lhs_ref, bu_ref, x_ref, yl_ref, *, steps, n_state):
    ls = S5_CHUNK_STEPS
    n_chunks = steps // ls
    ar = jnp.broadcast_to(ar_ref[...], (S5_SEGMENTS, n_state))
    ai = jnp.broadcast_to(ai_ref[...], (S5_SEGMENTS, n_state))

    rows_chunk = ls * S5_SEGMENTS

    def chunk_rows(c):
        return pl.ds(pl.multiple_of(c * rows_chunk, rows_chunk), rows_chunk)

    def project(c, gather_rows):
        if gather_rows:
            def gather(i, carry):
                lhs_ref[pl.ds(c * rows_chunk + i * S5_SEGMENTS, S5_SEGMENTS), :] = (
                    u_ref[pl.ds(c * ls + i, S5_SEGMENTS, stride=steps), :])
                return carry
            lax.fori_loop(0, ls, gather, 0, unroll=8)
        bu_ref[...] = jnp.dot(lhs_ref[chunk_rows(c), :].astype(BF16), bblk_ref[...],
                              preferred_element_type=F32)

    def recur(state, store):
        def step(i, carry):
            xr, xi = carry
            rows = pl.ds(i * S5_SEGMENTS, S5_SEGMENTS)
            nxr = ar * xr - ai * xi + bu_ref[rows, 0:n_state]
            nxi = ar * xi + ai * xr + bu_ref[rows, n_state:2 * n_state]
            if store:
                x_ref[rows, 0:n_state] = nxr
                x_ref[rows, n_state:2 * n_state] = nxi
            return nxr, nxi
        return lax.fori_loop(0, ls, step, state, unroll=2)

    zero = jnp.zeros((S5_SEGMENTS, n_state), F32)

    def pass1(c, state):
        project(c, True)
        return recur(state, False)
    er, ei = lax.fori_loop(0, n_chunks, pass1, (zero, zero))

    pr, pi = ar_ref[...], ai_ref[...]
    for _ in range(int(math.log2(steps))):
        pr, pi = pr * pr - pi * pi, 2.0 * pr * pi
    cr = jnp.zeros((1, n_state), F32)
    ci = jnp.zeros((1, n_state), F32)
    init_r, init_i = [], []
    for j in range(S5_SEGMENTS):
        init_r.append(cr)
        init_i.append(ci)
        cr, ci = (er[j:j + 1] + pr * cr - pi * ci, ei[j:j + 1] + pr * ci + pi * cr)
    x0 = (jnp.concatenate(init_r, axis=0), jnp.concatenate(init_i, axis=0))

    d = d_ref[...]

    def pass2(c, state):
        project(c, False)
        state = recur(state, True)
        xr = x_ref[:, 0:n_state].astype(BF16)
        xi = x_ref[:, n_state:2 * n_state].astype(BF16)
        y = (jnp.dot(xr, cre_ref[...], preferred_element_type=F32)
             - jnp.dot(xi, cim_ref[...], preferred_element_type=F32))
        yl_ref[...] = jax.nn.gelu(y + d * lhs_ref[chunk_rows(c), :])

        def emit(i, carry):
            y_ref[pl.ds(c * ls + i, S5_SEGMENTS, stride=steps), :] = (
                yl_ref[pl.ds(i * S5_SEGMENTS, S5_SEGMENTS), :])
            return carry
        lax.fori_loop(0, ls, emit, 0, unroll=8)
        return state
    lax.fori_loop(0, n_chunks, pass2, x0)


def s5_scan(u, ab_re, ab_im, bb_re, bb_im, c_re, c_im, d_skip):
    t, width = u.shape
    g, p = ab_re.shape
    gb = S5_GROUPS_PER_BLOCK
    n_blocks = g // gb
    n_state = gb * p
    steps = t // S5_SEGMENTS
    assert steps % S5_CHUNK_STEPS == 0 and steps & (steps - 1) == 0
    eye = jnp.eye(gb, dtype=F32)

    def blockdiag_in(bb):
        z = bb.reshape(n_blocks, gb, B_GROUP_CH, p)
        return jnp.einsum('bgcp,gh->bgchp', z, eye).reshape(n_blocks, gb * B_GROUP_CH, n_state)

    def blockdiag_out(cc):
        z = cc.reshape(n_blocks, gb, B_GROUP_CH, p)
        return jnp.einsum('bgcp,gh->bhpgc', z, eye).reshape(n_blocks, n_state, gb * B_GROUP_CH)

    bblk = jnp.concatenate([blockdiag_in(bb_re), blockdiag_in(bb_im)], axis=-1).astype(BF16)
    cre = blockdiag_out(c_re).astype(BF16)
    cim = blockdiag_out(c_im).astype(BF16)
    ar = ab_re.reshape(n_blocks, 1, n_state)
    ai = ab_im.reshape(n_blocks, 1, n_state)
    d = d_skip.reshape(n_blocks, 1, LANES)
    rows_chunk = S5_CHUNK_STEPS * S5_SEGMENTS
    return pl.pallas_call(
        functools.partial(_s5_scan_body, steps=steps, n_state=n_state),
        grid=(n_blocks,),
        in_specs=[pl.BlockSpec((t, LANES), lambda b: (0, b)),
                  pl.BlockSpec((None, LANES, 2 * n_state), lambda b: (b, 0, 0)),
                  pl.BlockSpec((None, n_state, LANES), lambda b: (b, 0, 0)),
                  pl.BlockSpec((None, n_state, LANES), lambda b: (b, 0, 0)),
                  pl.BlockSpec((None, 1, n_state), lambda b: (b, 0, 0)),
                  pl.BlockSpec((None, 1, n_state), lambda b: (b, 0, 0)),
                  pl.BlockSpec((None, 1, LANES), lambda b: (b, 0, 0))],
        out_specs=pl.BlockSpec((t, LANES), lambda b: (0, b)),
        out_shape=jax.ShapeDtypeStruct((t, width), F32),
        scratch_shapes=[pltpu.VMEM((t, LANES), F32),
                        pltpu.VMEM((rows_chunk, 2 * n_state), F32),
                        pltpu.VMEM((rows_chunk, 2 * n_state), F32),
                        pltpu.VMEM((rows_chunk, LANES), F32)],
        compiler_params=_cparams(("parallel",)),
        name="s5_scan",
    )(u, bblk, cre, cim, ar, ai, d)


def _attn_body(q_ref, kp_ref, kc_ref, vp_ref, vc_ref, o_ref, lse_ref, *, blocks_per_residue, n_heads):
    b = pl.program_id(0)
    not_first = (b % blocks_per_residue) != 0
    qi = lax.broadcasted_iota(jnp.int32, (C_QBLOCK, 2 * C_QBLOCK), 0)
    kj = lax.broadcasted_iota(jnp.int32, (C_QBLOCK, 2 * C_QBLOCK), 1)
    dist = C_QBLOCK + qi - kj
    valid = (dist >= 0) & (dist <= C_QBLOCK) & (not_first | (kj >= C_QBLOCK))
    scale = C_HEAD_DIM ** -0.5
    for h in range(n_heads):
        cols = slice(h * C_HEAD_DIM, (h + 1) * C_HEAD_DIM)
        q = q_ref[:, cols]
        k = jnp.concatenate([kp_ref[:, cols], kc_ref[:, cols]], axis=0)
        v = jnp.concatenate([vp_ref[:, cols], vc_ref[:, cols]], axis=0)
        s = lax.dot_general(q, k, (((1,), (1,)), ((), ())), preferred_element_type=F32) * scale
        s = jnp.where(valid, s, -jnp.inf)
        m = jnp.max(s, axis=-1, keepdims=True)
        e = jnp.exp(s - m)
        ssum = jnp.sum(e, axis=-1, keepdims=True)
        p = e / ssum
        o_ref[:, cols] = jnp.dot(p.astype(BF16), v, preferred_element_type=F32).astype(o_ref.dtype)
        lse_ref[:, h:h + 1] = m + jnp.log(ssum)


def window_attention(qkv, dil):
    t, w3 = qkv.shape
    width = w3 // 3
    n_heads = width // C_HEAD_DIM
    n_blocks = t // C_QBLOCK
    cur = lambda part: pl.BlockSpec((C_QBLOCK, width), lambda b, p=part: (b, p))
    prev = lambda part: pl.BlockSpec((C_QBLOCK, width), lambda b, p=part: (jnp.maximum(b - 1, 0), p))
    return pl.pallas_call(
        functools.partial(_attn_body, blocks_per_residue=n_blocks // dil, n_heads=n_heads),
        grid=(n_blocks,),
        in_specs=[cur(0), prev(1), cur(1), prev(2), cur(2)],
        out_specs=(pl.BlockSpec((C_QBLOCK, width), lambda b: (b, 0)),
                   pl.BlockSpec((C_QBLOCK, n_heads), lambda b: (b, 0))),
        out_shape=(jax.ShapeDtypeStruct((t, width), BF16),
                   jax.ShapeDtypeStruct((t, n_heads), F32)),
        compiler_params=_cparams(("parallel",)),
        name=f"window_attention_d{dil}",
    )(qkv, qkv, qkv, qkv, qkv)


def _residue_permutation(n, dil, inverse=False):
    row = lax.broadcasted_iota(jnp.int32, (n, n), 0)
    col = lax.broadcasted_iota(jnp.int32, (n, n), 1)
    per = n // dil
    if inverse:
        hit = col == (row % dil) * per + row // dil
    else:
        hit = col == (row % per) * dil + row // per
    return jnp.where(hit, 1.0, 0.0).astype(BF16)


def _residue_copies_body(x_ref, *o_refs, dils):
    x = x_ref[...]
    tm = x.shape[0]
    for dil, o_ref in zip(dils, o_refs):
        per = tm // dil
        shuffled = jnp.dot(_residue_permutation(tm, dil), x, preferred_element_type=F32).astype(BF16)
        for r in range(dil):
            o_ref[r] = shuffled[r * per:(r + 1) * per]


def residue_major_copies(x, dils, tm=256):
    t, d = x.shape
    outs = pl.pallas_call(
        functools.partial(_residue_copies_body, dils=dils),
        grid=(t // tm,),
        in_specs=[pl.BlockSpec((tm, d), lambda i: (i, 0))],
        out_specs=tuple(pl.BlockSpec((dil, tm // dil, d), lambda i: (0, i, 0)) for dil in dils),
        out_shape=tuple(jax.ShapeDtypeStruct((dil, t // dil, d), BF16) for dil in dils),
        compiler_params=_cparams(("parallel",)),
        name="residue_major_copies",
    )(x)
    return [o.reshape(t, d) for o in outs]


def _attn_combine_body(*refs, dils):
    n = len(dils)
    o_refs, l_refs, out_ref = refs[:n], refs[n:2 * n], refs[2 * n]
    scr_refs = refs[2 * n + 1:]
    tm, width = out_ref.shape
    nat = []
    scr = iter(scr_refs)
    for dil, o_ref in zip(dils, o_refs):
        if dil == 1:
            nat.append(o_ref.at[0])
            continue
        s_ref = next(scr)
        o2d = jnp.concatenate([o_ref[r] for r in range(dil)], axis=0)
        s_ref[...] = jnp.dot(_residue_permutation(tm, dil, inverse=True), o2d,
                             preferred_element_type=F32)
        nat.append(s_ref)
    ls = [l[...] for l in l_refs]
    mx = functools.reduce(jnp.maximum, ls)
    es = [jnp.exp(l - mx) for l in ls]
    den = functools.reduce(lambda a, b: a + b, es)
    alphas = [e / den for e in es]
    n_heads = ls[0].shape[1]
    for h in range(n_heads):
        cols = slice(h * C_HEAD_DIM, (h + 1) * C_HEAD_DIM)
        acc = None
        for alpha, o in zip(alphas, nat):
            term = alpha[:, h:h + 1] * o[:, cols].astype(F32)
            acc = term if acc is None else acc + term
        out_ref[:, cols] = acc.astype(out_ref.dtype)


def attn_combine(outs, lses, dils, tm=256):
    t, width = outs[0].shape
    n_heads = lses[0].shape[1]
    o_specs = [pl.BlockSpec((dil, tm // dil, width), lambda i: (0, i, 0)) for dil in dils]
    l_spec = pl.BlockSpec((tm, n_heads), lambda i: (i, 0))
    return pl.pallas_call(
        functools.partial(_attn_combine_body, dils=dils),
        grid=(t // tm,),
        in_specs=o_specs + [l_spec] * len(dils),
        out_specs=pl.BlockSpec((tm, width), lambda i: (i, 0)),
        out_shape=jax.ShapeDtypeStruct((t, width), BF16),
        scratch_shapes=[pltpu.VMEM((tm, width), F32) for dil in dils if dil > 1],
        compiler_params=_cparams(("parallel",)),
        name="attn_combine",
    )(*[o.reshape(dil, t // dil, width) for o, dil in zip(outs, dils)], *lses)


def _from_residue_major(a, dil):
    t, d = a.shape
    return a.reshape(dil, t // dil, d).transpose(1, 0, 2).reshape(t, d)


def _to_residue_major(a, dil):
    t, d = a.shape
    return a.reshape(t // dil, dil, d).transpose(1, 0, 2).reshape(t, d)


def ffn_layer(stream, gain, w_gate, w_up, w_down, layer, emit_stats):
    h, hb, ssq = stream
    ff = w_gate.shape[2]
    mid, w_down_bf16 = matmul_ws(hb, [w_gate, w_up], layer, [0, 0], ff, _epi_swiglu, BF16,
                                 norm=(ssq, gain), side=(w_down, layer, 256), tm=1024, tn=512,
                                 name="ffn_gate_up")
    tn = 512
    half = h.shape[1] // tn // 2
    down = functools.partial(matmul_rowres, mid, w_down_bf16, h, emit_stats=emit_stats, tm=1024, tn=tn,
                             tk=ff // 2, name="ffn_down")
    out = down(col_tiles=(half, half), prev=down(col_tiles=(0, half)))
    return out if emit_stats else (out[0], None, None)


def gmlp_layer(stream, gain, w_in, v_gain, w_spatial, b_spatial, w_out, layer):
    h, hb, ssq = stream
    d = h.shape[1]
    uv = matmul_ws(hb, [w_in], layer, [0], w_in.shape[2], _epi_gelu, BF16, norm=(ssq, gain),
                   tm=1024, tn=1024, name="gmlp_in")
    gated = gmlp_gate(uv, v_gain, w_spatial, b_spatial)
    return matmul_ws(gated, [w_out], layer, [0], d, _epi_res, F32, res=h, emit_stats=True,
                     tm=512, tn=1024, name="gmlp_out")


def s5_layer(stream, gain, w_in, lam_re, lam_im, log_step, b_re, b_im, c_re, c_im, d_skip, w_out,
             layer):
    h, hb, ssq = stream
    d = h.shape[1]
    u = matmul_ws(hb, [w_in], layer, [0], w_in.shape[2], _epi_id, F32, norm=(ssq, gain), tm=1024,
                  tn=min(1024, w_in.shape[2]), name="s5_in")
    ab_re, ab_im, bb_re, bb_im = s5_discretize(lam_re, lam_im, log_step, b_re, b_im)
    y = s5_scan(u, ab_re[::B_GROUP_CH], ab_im[::B_GROUP_CH], bb_re, bb_im, c_re, c_im, d_skip)
    return matmul_ws(y.astype(BF16), [w_out, w_out], layer, [0, d], d, _epi_glu_res, F32, res=h,
                     emit_stats=True, tm=1024, tn=1024, name="s5_out")


def attn_layer(stream, gain, w_in, w_out, layer):
    h, hb, ssq = stream
    d = h.shape[1]
    width = w_out.shape[1]
    dils = tuple(dil for _, dil in C_PATTERNS)
    assert all(window // dil == C_QBLOCK for window, dil in C_PATTERNS) and dils[0] == 1
    hbs = [hb] + residue_major_copies(hb, dils[1:])
    tn = 1024 if (3 * width) % 1024 == 0 else 512
    outs, lses = [], []
    for p_idx, dil in enumerate(dils):
        qkv = matmul_ws(hbs[p_idx], [w_in], layer, [p_idx * 3 * width], 3 * width, _epi_id, BF16,
                        norm=(_to_residue_major(ssq, dil), gain), tm=1024, tn=tn,
                        name=f"attn_in_d{dil}")
        o, lse = window_attention(qkv, dil)
        outs.append(o)
        lses.append(_from_residue_major(lse, dil))
    mixed = attn_combine(outs, lses, dils)
    return matmul_ws(mixed, [w_out], layer, [0], d, _epi_res, F32, res=h, emit_stats=True,
                     tm=1024, tn=1024, name="attn_out")


def kernel(x, norm_mix, norm_ffn, w_gate, w_up, w_down, a_w_in, a_v_gain, a_w_spatial, a_b_spatial,
           a_w_out, b_w_in, b_lambda_re, b_lambda_im, b_log_step, b_b_re, b_b_im, b_c_re, b_c_im,
           b_d_skip, b_w_out, c_w_in, c_w_out, final_norm):
    bsz, seq, d = x.shape
    depth = norm_mix.shape[0]
    outs = []
    for bi in range(bsz):
        stream = (x[bi],) + tuple(stream_stats(x[bi]))
        for i in range(depth):
            kind, j = i % 3, i // 3
            if kind == 0:
                stream = gmlp_layer(stream, norm_mix[i], a_w_in, a_v_gain[j], a_w_spatial[j],
                                    a_b_spatial[j], a_w_out, j)
            elif kind == 1:
                stream = s5_layer(stream, norm_mix[i], b_w_in, b_lambda_re[j], b_lambda_im[j],
                                  b_log_step[j], b_b_re[j], b_b_im[j], b_c_re[j], b_c_im[j],
                                  b_d_skip[j], b_w_out, j)
            else:
                stream = attn_layer(stream, norm_mix[i], c_w_in, c_w_out, j)
            stream = ffn_layer(stream, norm_ffn[i], w_gate, w_up, w_down, i,
                               emit_stats=i + 1 < depth)
        outs.append(rmsnorm(stream[0], final_norm, x.dtype))
    return jnp.stack(outs)
```
